```python
import math
import jax, jax.numpy as jnp
from jax import lax
import numpy as np

D_MODEL = 1024
BATCH = 2
SEQ = 8192
DEPTH = 1

EPS = 1e-6
D_FF = 2816
N_MOD = 9
GM_CHUNK = 128
GM_WIDTH = D_MODEL
GM_GROUPS = 8
GM_GROUP_DIM = GM_WIDTH // GM_GROUPS
DN_HEADS = 8
DN_HEAD_DIM = 128
DN_WIDTH = DN_HEADS * DN_HEAD_DIM
DN_CONV = 4
DN_CHUNK = 64
N_BRANCH = 2
IN_SPLITS = (GM_WIDTH, GM_WIDTH, DN_WIDTH, DN_WIDTH, DN_WIDTH, DN_WIDTH,
             DN_HEADS, DN_HEADS, N_BRANCH * D_MODEL)
IN_WIDTH = sum(IN_SPLITS)
IN_OFFSETS = tuple(int(o) for o in np.cumsum(IN_SPLITS)[:-1])

kernel_name = "hybrid_gmlp_gdn_macaron_adaln"


def rms_norm(x, g):
    xf = x.astype(jnp.float32)
    y = xf * lax.rsqrt(jnp.mean(xf * xf, axis=-1, keepdims=True) + EPS)
    return (y * g.astype(jnp.float32)).astype(x.dtype)


def layer_norm(x, g, b):
    xf = x.astype(jnp.float32)
    mu = jnp.mean(xf, axis=-1, keepdims=True)
    var = jnp.mean(jnp.square(xf - mu), axis=-1, keepdims=True)
    y = (xf - mu) * lax.rsqrt(var + EPS)
    return (y * g.astype(jnp.float32) + b.astype(jnp.float32)).astype(x.dtype)


def modulate(h, shift, scale):
    return h * (1.0 + scale[:, None, :]) + shift[:, None, :]


def swiglu(h, w_in, w_out):
    a, b = jnp.split(h @ w_in, 2, axis=-1)
    return (jax.nn.silu(a) * b) @ w_out


def l2_normalize(x):
    return x * lax.rsqrt(jnp.sum(x * x, axis=-1, keepdims=True) + EPS)


def causal_depthwise_conv(x, w):
    ch = x.shape[-1]
    return lax.conv_general_dilated(
        x, w[:, None, :].astype(x.dtype), window_strides=(1,),
        padding=[(DN_CONV - 1, 0)], dimension_numbers=('NWC', 'WIO', 'NWC'),
        feature_group_count=ch)


def chunked_spatial_gating(u, v, ln_g, ln_b, w_s, b_s):
    bsz, seq, _ = v.shape
    v = layer_norm(v, ln_g, ln_b)
    vc = v.reshape(bsz, seq // GM_CHUNK, GM_CHUNK, GM_GROUPS, GM_GROUP_DIM)
    causal = jnp.tril(jnp.ones((GM_CHUNK, GM_CHUNK), dtype=bool))
    w = jnp.where(causal, w_s, 0.0).astype(v.dtype)
    mixed = jnp.einsum('gts,bnsgc->bntgc', w, vc) + b_s.T[:, :, None]
    return u * mixed.reshape(bsz, seq, GM_WIDTH)


def chunk_gated_delta_rule(q, k, v, g, beta):
    bsz, seq, nh, dk = q.shape
    dv = v.shape[-1]
    C = DN_CHUNK
    n_chunks = seq // C

    def chunks(t):
        return t.reshape(bsz, n_chunks, C, nh, t.shape[-1]).transpose(1, 0, 3, 2, 4)

    q, k, v = chunks(q), chunks(k), chunks(v)
    g = chunks(g[..., None])[..., 0]
    beta = chunks(beta[..., None])
    g_cum = jnp.cumsum(g, axis=-1)
    causal = jnp.tril(jnp.ones((C, C), dtype=bool))
    strict = jnp.tril(jnp.ones((C, C), dtype=bool), -1)
    decay = jnp.exp(jnp.where(causal, g_cum[..., :, None] - g_cum[..., None, :], -jnp.inf))
    k_beta = k * beta
    a_low = jnp.where(strict, jnp.einsum('nbhcd,nbhed->nbhce', k_beta, k) * decay, 0.0)
    tri = a_low + jnp.eye(C, dtype=q.dtype)
    rhs = jnp.concatenate([v * beta, k_beta * jnp.exp(g_cum)[..., None]], axis=-1)
    sol = lax.linalg.triangular_solve(tri, rhs, left_side=True, lower=True, unit_diagonal=True)
    u, w = sol[..., :dv], sol[..., dv:]
    qk = jnp.einsum('nbhcd,nbhed->nbhce', q, k) * decay
    q_dec = q * jnp.exp(g_cum)[..., None]
    k_dec = k * jnp.exp(g_cum[..., -1:] - g_cum)[..., None]
    g_last = jnp.exp(g_cum[..., -1])[..., None, None]

    def step(state, xs):
        u_n, w_n, qk_n, qd_n, kd_n, gl_n = xs
        v_new = u_n - jnp.einsum('bhck,bhkv->bhcv', w_n, state)
        o_n = (jnp.einsum('bhck,bhkv->bhcv', qd_n, state)
               + jnp.einsum('bhce,bhev->bhcv', qk_n, v_new))
        state = state * gl_n + jnp.einsum('bhck,bhcv->bhkv', kd_n, v_new)
        return state, o_n

    state0 = jnp.zeros((bsz, nh, dk, dv), q.dtype)
    _, o = lax.scan(step, state0, (u, w, qk, q_dec, k_dec, g_last))
    return o.transpose(1, 0, 3, 2, 4).reshape(bsz, seq, nh, dv)


def gated_deltanet(q, k, v, z, a, b, conv_w, a_log, dt_bias, norm_g):
    bsz, seq, _ = q.shape
    f32 = jnp.float32
    qkv = jax.nn.silu(causal_depthwise_conv(jnp.concatenate([q, k, v], axis=-1), conv_w))
    q, k, v = jnp.split(qkv, 3, axis=-1)
    shp = (bsz, seq, DN_HEADS, DN_HEAD_DIM)
    q = l2_normalize(q.reshape(shp).astype(f32)) * (DN_HEAD_DIM ** -0.5)
    k = l2_normalize(k.reshape(shp).astype(f32))
    v = v.reshape(shp).astype(f32)
    beta = jax.nn.sigmoid(b.astype(f32))
    g = -jnp.exp(a_log.astype(f32)) * jax.nn.softplus(a.astype(f32) + dt_bias.astype(f32))
    o = chunk_gated_delta_rule(q, k, v, g, beta)
    o = o * lax.rsqrt(jnp.mean(o * o, axis=-1, keepdims=True) + EPS) * norm_g.astype(f32)
    o = o * jax.nn.silu(z.reshape(shp).astype(f32))
    return o.reshape(bsz, seq, DN_WIDTH).astype(z.dtype)


def setup_inputs(seed: int = 0) -> dict:
    key = jax.random.key(seed)
    ks = iter(jax.random.split(key, 32))
    f32 = jnp.float32

    def nrm(shape, fan_in, gain=1.0):
        return jax.random.normal(next(ks), shape, f32) * (gain * fan_in ** -0.5)

    def gain_vec(shape):
        return 1.0 + 0.02 * jax.random.normal(next(ks), shape, f32)

    L = DEPTH
    x = jax.random.normal(next(ks), (BATCH, SEQ, D_MODEL), f32)
    c = jax.random.normal(next(ks), (BATCH, D_MODEL), f32)
    w_ada = nrm((L, D_MODEL, N_MOD * D_MODEL), D_MODEL, 0.5)
    b_ada = 0.01 * jax.random.normal(next(ks), (L, N_MOD * D_MODEL), f32)
    norm1_g = gain_vec((L, D_MODEL))
    ffn1_w_in = nrm((L, D_MODEL, 2 * D_FF), D_MODEL)
    ffn1_w_out = nrm((L, D_FF, D_MODEL), D_FF)
    norm2_g = gain_vec((L, D_MODEL))
    w_in = nrm((L, D_MODEL, IN_WIDTH), D_MODEL)
    conv_w = nrm((L, DN_CONV, 3 * DN_WIDTH), DN_CONV)
    a_log = jnp.log(jax.random.uniform(next(ks), (L, DN_HEADS), f32, 1.0, 16.0))
    dt = jnp.exp(jax.random.uniform(next(ks), (L, DN_HEADS), f32,
                                    math.log(1e-3), math.log(1e-1)))
    dt_bias = dt + jnp.log(-jnp.expm1(-dt))
    dn_norm_g = gain_vec((L, DN_HEAD_DIM))
    gm_ln_g = gain_vec((L, GM_WIDTH))
    gm_ln_b = 0.01 * jax.random.normal(next(ks), (L, GM_WIDTH), f32)
    gm_w_s = nrm((L, GM_GROUPS, GM_CHUNK, GM_CHUNK), GM_CHUNK)
    gm_b_s = 1.0 + 0.02 * jax.random.normal(next(ks), (L, GM_GROUPS, GM_CHUNK), f32)
    w_branch = nrm((L, N_BRANCH, GM_WIDTH, D_MODEL), GM_WIDTH)
    w_out = nrm((L, D_MODEL, D_MODEL), D_MODEL)
    norm3_g = gain_vec((L, D_MODEL))
    ffn2_w_in = nrm((L, D_MODEL, 2 * D_FF), D_MODEL)
    ffn2_w_out = nrm((L, D_FF, D_MODEL), D_FF)
    final_g = gain_vec((D_MODEL,))
    return {"x": x, "c": c, "w_ada": w_ada, "b_ada": b_ada,
            "norm1_g": norm1_g, "ffn1_w_in": ffn1_w_in, "ffn1_w_out": ffn1_w_out,
            "norm2_g": norm2_g, "w_in": w_in, "conv_w": conv_w, "a_log": a_log,
            "dt_bias": dt_bias, "dn_norm_g": dn_norm_g, "gm_ln_g": gm_ln_g,
            "gm_ln_b": gm_ln_b, "gm_w_s": gm_w_s, "gm_b_s": gm_b_s,
            "w_branch": w_branch, "w_out": w_out, "norm3_g": norm3_g,
            "ffn2_w_in": ffn2_w_in, "ffn2_w_out": ffn2_w_out, "final_g": final_g}


def reference(x, c, w_ada, b_ada, norm1_g, ffn1_w_in, ffn1_w_out, norm2_g, w_in, conv_w,
              a_log, dt_bias, dn_norm_g, gm_ln_g, gm_ln_b, gm_w_s, gm_b_s, w_branch,
              w_out, norm3_g, ffn2_w_in, ffn2_w_out, final_g):
    c_act = jax.nn.silu(c)
    for l in range(DEPTH):
        mod = c_act @ w_ada[l] + b_ada[l]
        sh1, sc1, ga1, sh2, sc2, ga2, sh3, sc3, ga3 = jnp.split(mod, N_MOD, axis=-1)

        h = modulate(rms_norm(x, norm1_g[l]), sh1, sc1)
        x = x + 0.5 * ga1[:, None, :] * swiglu(h, ffn1_w_in[l], ffn1_w_out[l])

        h = modulate(rms_norm(x, norm2_g[l]), sh2, sc2)
        proj = h @ w_in[l]
        u_a, v_a, q_b, k_b, v_b, z_b, a_b, b_b, gate_logits = jnp.split(proj, IN_OFFSETS, axis=-1)
        u_a = jax.nn.gelu(u_a, approximate=False)
        v_a = jax.nn.gelu(v_a, approximate=False)
        o_a = chunked_spatial_gating(u_a, v_a, gm_ln_g[l], gm_ln_b[l], gm_w_s[l], gm_b_s[l])
        o_b = gated_deltanet(q_b, k_b, v_b, z_b, a_b, b_b, conv_w[l], a_log[l],
                             dt_bias[l], dn_norm_g[l])
        g_a, g_b = jnp.split(jax.nn.sigmoid(gate_logits), N_BRANCH, axis=-1)
        merged = g_a * (o_a @ w_branch[l, 0]) + g_b * (o_b @ w_branch[l, 1])
        x = x + ga2[:, None, :] * (merged @ w_out[l])

        h = modulate(rms_norm(x, norm3_g[l]), sh3, sc3)
        x = x + 0.5 * ga3[:, None, :] * swiglu(h, ffn2_w_in[l], ffn2_w_out[l])
    return rms_norm(x, final_g)
```

```python
import functools

import jax
import jax.numpy as jnp
from jax import lax
from jax.experimental import pallas as pl
from jax.experimental.pallas import tpu as pltpu

F32 = jnp.float32
BF16 = jnp.bfloat16

EPS = 1e-6
N_MOD = 9
LANES = 128
GM_CHUNK = 128
GM_GROUPS = 8
DN_HEADS = 8
DN_HEAD_DIM = 128
DN_CONV = 4
DN_CHUNK = 64
CONV_HALO = 8
VMEM_LIMIT_BYTES = 56 * 1024 * 1024


def _dot(a, b):
    return jnp.dot(a, b, preferred_element_type=F32)


def _dot_nt(a, b):
    return lax.dot_general(a, b, (((1,), (1,)), ((), ())), preferred_element_type=F32)


def _dot_tn(a, b):
    return lax.dot_general(a, b, (((0,), (0,)), ((), ())), preferred_element_type=F32)


def _split3(x):
    hi = x.astype(BF16)
    r1 = x - hi.astype(F32)
    mid = r1.astype(BF16)
    lo = (r1 - mid.astype(F32)).astype(BF16)
    return hi, mid, lo


def _dot_sel_left(sel, x):
    hi, mid, lo = _split3(x)
    return _dot(sel, hi) + _dot(sel, mid) + _dot(sel, lo)


def _dot_sel_right(x, sel):
    hi, mid, lo = _split3(x)
    return _dot(hi, sel) + _dot(mid, sel) + _dot(lo, sel)


def _sigmoid(x):
    return jax.nn.sigmoid(x)


def _silu(x):
    return x * _sigmoid(x)


def _gelu_exact(x):
    return 0.5 * x * (1.0 + lax.erf(x * (2.0 ** -0.5)))


def _rms(x, g):
    return x * lax.rsqrt(jnp.mean(x * x, axis=-1, keepdims=True) + EPS) * g


def _mod_slices(mod_ref, k, d):
    sh = mod_ref[:, (3 * k) * d:(3 * k + 1) * d]
    sc = mod_ref[:, (3 * k + 1) * d:(3 * k + 2) * d]
    ga = mod_ref[:, (3 * k + 2) * d:(3 * k + 3) * d]
    return sh, sc, ga


def _ada_kernel(c_ref, w_ref, b_ref, o_ref):
    ca = _silu(c_ref[...]).astype(BF16)
    o_ref[...] = _dot(ca, w_ref[...].astype(BF16)) + b_ref[...]


def _ada(c, w_ada, b_ada):
    bsz, d = c.shape
    n = w_ada.shape[1]
    tn = 1024
    c8 = jnp.zeros((8, d), F32).at[:bsz].set(c)
    out = pl.pallas_call(
        _ada_kernel,
        out_shape=jax.ShapeDtypeStruct((8, n), F32),
        grid=(n // tn,),
        in_specs=[pl.BlockSpec((8, d), lambda j: (0, 0)),
                  pl.BlockSpec((d, tn), lambda j: (0, j)),
                  pl.BlockSpec((1, tn), lambda j: (0, j))],
        out_specs=pl.BlockSpec((8, tn), lambda j: (0, j)),
        compiler_params=pltpu.CompilerParams(
            dimension_semantics=("arbitrary",), vmem_limit_bytes=VMEM_LIMIT_BYTES),
        name="ada_mod",
    )(c8, w_ada, b_ada.reshape(1, n))
    return out[:bsz]


def _ffn_kernel(x_ref, mod_ref, g_ref, win_ref, wout_ref, *rest, k, n_split, final):
    if final:
        fg_ref, o_ref = rest
    else:
        (o_ref,) = rest
    d = x_ref.shape[-1]
    f = wout_ref.shape[0]
    fc = f // n_split
    x = x_ref[...]
    sh, sc, ga = _mod_slices(mod_ref, k, d)
    h = _rms(x, g_ref[...]) * (1.0 + sc) + sh
    hb = h.astype(BF16)
    acc = None
    for j in range(n_split):
        a = _dot(hb, win_ref[:, j * fc:(j + 1) * fc])
        b = _dot(hb, win_ref[:, f + j * fc:f + (j + 1) * fc])
        act = (_silu(a) * b).astype(BF16)
        part = _dot(act, wout_ref[j * fc:(j + 1) * fc, :])
        acc = part if acc is None else acc + part
    y = x + (0.5 * ga) * acc
    if final:
        y = _rms(y, fg_ref[...])
    o_ref[...] = y


def _const_spec(shape):
    nd = len(shape)
    return pl.BlockSpec(shape, lambda b, i: (0,) * nd, pipeline_mode=pl.Buffered(1))


def _ffn(x, mod3, norm_g, w_in, w_out, final_g, *, k, tm):
    bsz, seq, d = x.shape
    f = w_out.shape[0]
    final = final_g is not None
    tok = pl.BlockSpec((None, tm, d), lambda b, i: (b, i, 0))
    in_specs = [tok,
                pl.BlockSpec((None, 1, N_MOD * d), lambda b, i: (b, 0, 0)),
                _const_spec((1, d)),
                _const_spec((d, 2 * f)),
                _const_spec((f, d))]
    args = [x, mod3, norm_g.reshape(1, d), w_in, w_out]
    if final:
        in_specs.append(_const_spec((1, d)))
        args.append(final_g.reshape(1, d))
    return pl.pallas_call(
        functools.partial(_ffn_kernel, k=k, n_split=2, final=final),
        out_shape=jax.ShapeDtypeStruct((bsz, seq, d), F32),
        grid=(bsz, seq // tm),
        in_specs=in_specs,
        out_specs=tok,
        compiler_params=pltpu.CompilerParams(
            dimension_semantics=("parallel", "parallel"),
            vmem_limit_bytes=VMEM_LIMIT_BYTES),
        name="ffn_final" if final else "ffn",
    )(*args)


def _mix_in_kernel(x_ref, mod_ref, g2_ref, wuv_ref, wqkv_ref, wz_ref, wab_ref, wgate_ref,
                   convw_ref, alog_ref, dtb_ref, lng_ref, lnb_ref, ws_ref, bsf_ref,
                   oa_ref, qkv_ref, sz_ref, gate_ref, gb_ref, cbuf_ref):
    tm, d = x_ref.shape
    width = oa_ref.shape[-1]
    dnw = sz_ref.shape[-1]

    @pl.when(pl.program_id(1) == 0)
    def _():
        cbuf_ref[0:CONV_HALO, :] = jnp.zeros((CONV_HALO, cbuf_ref.shape[1]), F32)

    x = x_ref[...]
    sh, sc, _ = _mod_slices(mod_ref, 1, d)
    hb = (_rms(x, g2_ref[...]) * (1.0 + sc) + sh).astype(BF16)

    uv = _dot(hb, wuv_ref[...])
    u = _gelu_exact(uv[:, :width])
    v = _gelu_exact(uv[:, width:])
    mu = jnp.mean(v, axis=-1, keepdims=True)
    vc = v - mu
    var = jnp.mean(vc * vc, axis=-1, keepdims=True)
    vb = (vc * lax.rsqrt(var + EPS) * lng_ref[...] + lnb_ref[...]).astype(BF16)
    row = lax.broadcasted_iota(jnp.int32, (GM_CHUNK, GM_CHUNK), 0)
    col = lax.broadcasted_iota(jnp.int32, (GM_CHUNK, GM_CHUNK), 1)
    tril = row >= col
    gd = width // GM_GROUPS
    for g in range(GM_GROUPS):
        wm = jnp.where(tril, ws_ref[g], 0.0).astype(BF16)
        bias = bsf_ref[:, g * gd:(g + 1) * gd]
        for c in range(tm // GM_CHUNK):
            rs = slice(c * GM_CHUNK, (c + 1) * GM_CHUNK)
            cs = slice(g * gd, (g + 1) * gd)
            mixed = _dot(wm, vb[rs, cs]) + bias
            oa_ref[rs, cs] = (u[rs, cs] * mixed).astype(oa_ref.dtype)

    pre = _dot(hb, wqkv_ref[...])
    cbuf_ref[CONV_HALO:CONV_HALO + tm, :] = pre
    conv = pre * convw_ref[DN_CONV - 1:DN_CONV, :]
    for j in range(DN_CONV - 1):
        off = CONV_HALO - (DN_CONV - 1) + j
        conv = conv + cbuf_ref[off:off + tm, :] * convw_ref[j:j + 1, :]
    cbuf_ref[0:CONV_HALO, :] = pre[tm - CONV_HALO:tm, :]
    act = _silu(conv)
    for h in range(DN_HEADS):
        for part, scale in ((0, DN_HEAD_DIM ** -0.5), (1, 1.0)):
            cs = slice(part * dnw + h * DN_HEAD_DIM, part * dnw + (h + 1) * DN_HEAD_DIM)
            t = act[:, cs]
            t = t * lax.rsqrt(jnp.sum(t * t, axis=-1, keepdims=True) + EPS)
            if part == 0:
                t = t * scale
            qkv_ref[:, cs] = t
    qkv_ref[:, 2 * dnw:] = act[:, 2 * dnw:]

    sz_ref[...] = _silu(_dot(hb, wz_ref[...]))
    gate_ref[...] = _sigmoid(_dot(hb, wgate_ref[...]))
    ab = _dot(hb, wab_ref[...])
    gb_ref[:, :LANES] = -jnp.exp(alog_ref[...]) * jax.nn.softplus(ab[:, :LANES] + dtb_ref[...])
    gb_ref[:, LANES:] = _sigmoid(ab[:, LANES:])


def _mix_in(x, mod3, norm_g, wuv, wqkv, wz, wab, wgate, conv_w, alog, dtb, lng, lnb,
            w_s, bsf, *, tm):
    bsz, seq, d = x.shape
    width = wuv.shape[1] // 2
    dnw = wz.shape[1]

    def tok(n):
        return pl.BlockSpec((None, tm, n), lambda b, i: (b, i, 0))

    in_specs = [tok(d),
                pl.BlockSpec((None, 1, N_MOD * d), lambda b, i: (b, 0, 0)),
                _const_spec((1, d)),
                _const_spec(wuv.shape), _const_spec(wqkv.shape), _const_spec(wz.shape),
                _const_spec(wab.shape), _const_spec(wgate.shape),
                _const_spec(conv_w.shape), _const_spec(alog.shape), _const_spec(dtb.shape),
                _const_spec(lng.shape), _const_spec(lnb.shape),
                _const_spec(w_s.shape), _const_spec(bsf.shape)]
    out_shape = (jax.ShapeDtypeStruct((bsz, seq, width), BF16),
                 jax.ShapeDtypeStruct((bsz, seq, 3 * dnw), F32),
                 jax.ShapeDtypeStruct((bsz, seq, dnw), F32),
                 jax.ShapeDtypeStruct((bsz, seq, 2 * d), F32),
                 jax.ShapeDtypeStruct((bsz, seq, 2 * LANES), F32))
    out_specs = (tok(width), tok(3 * dnw), tok(dnw), tok(2 * d), tok(2 * LANES))
    return pl.pallas_call(
        _mix_in_kernel,
        out_shape=out_shape,
        grid=(bsz, seq // tm),
        in_specs=in_specs,
        out_specs=out_specs,
        scratch_shapes=[pltpu.VMEM((tm + CONV_HALO, 3 * dnw), F32)],
        compiler_params=pltpu.CompilerParams(
            dimension_semantics=("arbitrary", "arbitrary"),
            vmem_limit_bytes=VMEM_LIMIT_BYTES),
        name="mix_in",
    )(x, mod3, norm_g.reshape(1, d), wuv, wqkv, wz, wab, wgate, conv_w, alog, dtb,
      lng, lnb, w_s, bsf)


def _unit_lower_inverse_minus_eye(a):
    x_b = a.astype(BF16)
    n = -a
    p = 2
    while p < DN_CHUNK:
        x = _dot(x_b, x_b)
        x_b = x.astype(BF16)
        n = n + x + _dot(n.astype(BF16), x_b)
        p *= 2
    return n


def _dn_out_kernel(qkv_ref, gb_ref, oa_ref, sz_ref, gate_ref, x_ref, mod_ref, ng_ref,
                   wb0_ref, wb1_ref, wo_ref, o_ref, s_ref, ob_ref):
    tm, d = x_ref.shape
    dnw = sz_ref.shape[-1]
    hd = DN_HEAD_DIM
    n_chunks = tm // DN_CHUNK

    @pl.when(pl.program_id(1) == 0)
    def _():
        s_ref[...] = jnp.zeros(s_ref.shape, F32)

    g = gb_ref[:, :LANES]
    beta = gb_ref[:, LANES:]
    row = lax.broadcasted_iota(jnp.int32, (tm, tm), 0)
    col = lax.broadcasted_iota(jnp.int32, (tm, tm), 1)
    shift = DN_CHUNK.bit_length() - 1
    same = jnp.right_shift(row, shift) == jnp.right_shift(col, shift)
    causal = jnp.logical_and(same, col <= row)
    strict = jnp.logical_and(same, col < row)
    cum_sel = jnp.where(causal, 1.0, 0.0).astype(BF16)
    e_row = lax.broadcasted_iota(jnp.int32, (LANES, dnw), 0)
    e_col = lax.broadcasted_iota(jnp.int32, (LANES, dnw), 1)
    expand = jnp.where(jnp.right_shift(e_col, hd.bit_length() - 1) == e_row,
                       1.0, 0.0).astype(BF16)

    gc = _dot_sel_left(cum_sel, g)
    g_full = _dot_sel_right(gc, expand)
    b_full = _dot_sel_right(beta, expand)
    gc_t = gc.T
    eg_full = jnp.exp(g_full)

    for h in range(DN_HEADS):
        cs = slice(h * hd, (h + 1) * hd)
        q = qkv_ref[:, h * hd:(h + 1) * hd]
        k = qkv_ref[:, dnw + h * hd:dnw + (h + 1) * hd]
        v = qkv_ref[:, 2 * dnw + h * hd:2 * dnw + (h + 1) * hd]
        gh = g_full[:, cs]
        bh = b_full[:, cs]
        egh = eg_full[:, cs]
        kb = k * bh
        k_b = k.astype(BF16)
        diff = gh - gc_t[h:h + 1, :]
        decay = jnp.where(causal, jnp.exp(jnp.where(causal, diff, 0.0)), 0.0)
        a_low = jnp.where(strict, _dot_nt(kb.astype(BF16), k_b) * decay, 0.0)
        inv_m1 = _unit_lower_inverse_minus_eye(a_low)
        rhs = jnp.concatenate([v * bh, kb * egh], axis=1)
        sol = rhs + _dot(inv_m1.astype(BF16), rhs.astype(BF16))
        u = sol[:, :hd]
        w_b = sol[:, hd:].astype(BF16)
        qd_b = (q * egh).astype(BF16)
        qk_b = (_dot_nt(q.astype(BF16), k_b) * decay).astype(BF16)

        o_inter = []
        v_new = []
        for n in range(n_chunks):
            rs = slice(n * DN_CHUNK, (n + 1) * DN_CHUNK)
            last = (n + 1) * DN_CHUNK - 1
            s_h = s_ref[h]
            s_b = s_h.astype(BF16)
            vn = u[rs] - _dot(w_b[rs], s_b)
            o_inter.append(_dot(qd_b[rs], s_b))
            kd_b = (k[rs] * jnp.exp(gh[last:last + 1, :] - gh[rs])).astype(BF16)
            vn_b = vn.astype(BF16)
            s_ref[h] = s_h * egh[last:last + 1, :] + _dot_tn(kd_b, vn_b)
            v_new.append(vn_b)
        o_h = jnp.concatenate(o_inter, axis=0) + _dot(qk_b, jnp.concatenate(v_new, axis=0))
        o_h = o_h * lax.rsqrt(jnp.mean(o_h * o_h, axis=-1, keepdims=True) + EPS) * ng_ref[...]
        ob_ref[:, cs] = (o_h * sz_ref[:, cs]).astype(ob_ref.dtype)

    _, _, ga = _mod_slices(mod_ref, 1, d)
    merged = (gate_ref[:, :d] * _dot(oa_ref[...], wb0_ref[...])
              + gate_ref[:, d:] * _dot(ob_ref[...], wb1_ref[...]))
    o_ref[...] = x_ref[...] + ga * _dot(merged.astype(BF16), wo_ref[...])


def _dn_out(qkv, gb, oa, sz, gate, x, mod3, ng, wb0, wb1, wo, *, tm):
    bsz, seq, d = x.shape
    dnw = sz.shape[-1]

    def tok(n):
        return pl.BlockSpec((None, tm, n), lambda b, i: (b, i, 0))

    in_specs = [tok(3 * dnw), tok(2 * LANES), tok(oa.shape[-1]), tok(dnw), tok(2 * d), tok(d),
                pl.BlockSpec((None, 1, N_MOD * d), lambda b, i: (b, 0, 0)),
                _const_spec(ng.shape), _const_spec(wb0.shape), _const_spec(wb1.shape),
                _const_spec(wo.shape)]
    return pl.pallas_call(
        _dn_out_kernel,
        out_shape=jax.ShapeDtypeStruct((bsz, seq, d), F32),
        grid=(bsz, seq // tm),
        in_specs=in_specs,
        out_specs=tok(d),
        scratch_shapes=[pltpu.VMEM((DN_HEADS, DN_HEAD_DIM, DN_HEAD_DIM), F32),
                        pltpu.VMEM((tm, dnw), BF16)],
        compiler_params=pltpu.CompilerParams(
            dimension_semantics=("arbitrary", "arbitrary"),
            vmem_limit_bytes=VMEM_LIMIT_BYTES),
        name="dn_out",
    )(qkv, gb, oa, sz, gate, x, mod3, ng, wb0, wb1, wo)


def _pad_lanes(w):
    pad = LANES - w.shape[-1]
    return jnp.pad(w, [(0, 0)] * (w.ndim - 1) + [(0, pad)])


def kernel(x, c, w_ada, b_ada, norm1_g, ffn1_w_in, ffn1_w_out, norm2_g, w_in, conv_w, a_log,
           dt_bias, dn_norm_g, gm_ln_g, gm_ln_b, gm_w_s, gm_b_s, w_branch, w_out, norm3_g,
           ffn2_w_in, ffn2_w_out, final_g):
    bsz, seq, d = x.shape
    depth = w_ada.shape[0]
    width = gm_ln_g.shape[-1]
    dnw = DN_HEADS * DN_HEAD_DIM
    tm_ffn = min(512, seq)
    tm_mix = min(256, seq)
    tm_dn = min(128, seq)

    for l in range(depth):
        mod3 = _ada(c, w_ada[l], b_ada[l]).reshape(bsz, 1, N_MOD * d)
        x = _ffn(x, mod3, norm1_g[l], ffn1_w_in[l].astype(BF16), ffn1_w_out[l].astype(BF16),
                 None, k=0, tm=tm_ffn)

        wl = w_in[l]
        o = 0
        wuv = wl[:, o:o + 2 * width].astype(BF16); o += 2 * width
        wqkv = wl[:, o:o + 3 * dnw].astype(BF16); o += 3 * dnw
        wz = wl[:, o:o + dnw].astype(BF16); o += dnw
        wa = wl[:, o:o + DN_HEADS]; o += DN_HEADS
        wb = wl[:, o:o + DN_HEADS]; o += DN_HEADS
        wgate = wl[:, o:].astype(BF16)
        wab = jnp.concatenate([_pad_lanes(wa), _pad_lanes(wb)], axis=1).astype(BF16)
        bsf = jnp.repeat(gm_b_s[l].T, width // GM_GROUPS, axis=1)
        oa, qkv, sz, gate, gb = _mix_in(
            x, mod3, norm2_g[l], wuv, wqkv, wz, wab, wgate, conv_w[l],
            _pad_lanes(a_log[l].reshape(1, -1)), _pad_lanes(dt_bias[l].reshape(1, -1)),
            gm_ln_g[l].reshape(1, -1), gm_ln_b[l].reshape(1, -1), gm_w_s[l], bsf, tm=tm_mix)
        x = _dn_out(qkv, gb, oa, sz, gate, x, mod3, dn_norm_g[l].reshape(1, -1),
                    w_branch[l, 0].astype(BF16), w_branch[l, 1].astype(BF16),
                    w_out[l].astype(BF16), tm=tm_dn)

        last = l == depth - 1
        x = _ffn(x, mod3, norm3_g[l], ffn2_w_in[l].astype(BF16), ffn2_w_out[l].astype(BF16),
                 final_g if last else None, k=2, tm=tm_ffn)
    if depth == 0:
        raise ValueError("depth must be >= 1")
    return x
```

```python
import functools

import jax
import jax.numpy as jnp
from jax import lax
from jax.experimental import pallas as pl
from jax.experimental.pallas import tpu as pltpu

F32 = jnp.float32
BF16 = jnp.bfloat16

EPS = 1e-6
N_MOD = 9
LANES = 128
GM_CHUNK = 128
GM_GROUPS = 8
DN_HEADS = 8
DN_HEAD_DIM = 128
DN_CONV = 4
DN_CHUNK = 64
CONV_HALO = 8
VMEM_LIMIT_BYTES = 56 * 1024 * 1024


def _dot(a, b):
    return jnp.dot(a, b, preferred_element_type=F32)


def _dot_nt(a, b):
    return lax.dot_general(a, b, (((1,), (1,)), ((), ())), preferred_element_type=F32)


def _dot_tn(a, b):
    return lax.dot_general(a, b, (((0,), (0,)), ((), ())), preferred_element_type=F32)


def _split3(x):
    hi = x.astype(BF16)
    r1 = x - hi.astype(F32)
    mid = r1.astype(BF16)
    lo = (r1 - mid.astype(F32)).astype(BF16)
    return hi, mid, lo


def _dot_sel_left(sel, x):
    hi, mid, lo = _split3(x)
    return _dot(sel, hi) + _dot(sel, mid) + _dot(sel, lo)


def _dot_sel_right(x, sel):
    hi, mid, lo = _split3(x)
    return _dot(hi, sel) + _dot(mid, sel) + _dot(lo, sel)


def _sigmoid(x):
    return jax.nn.sigmoid(x)


def _silu(x):
    return x * _sigmoid(x)


def _gelu_exact(x):
    return 0.5 * x * (1.0 + lax.erf(x * (2.0 ** -0.5)))


def _rms(x, g):
    return x * lax.rsqrt(jnp.mean(x * x, axis=-1, keepdims=True) + EPS) * g


def _mod_slices(mod_ref, k, d):
    sh = mod_ref[:, (3 * k) * d:(3 * k + 1) * d]
    sc = mod_ref[:, (3 * k + 1) * d:(3 * k + 2) * d]
    ga = mod_ref[:, (3 * k + 2) * d:(3 * k + 3) * d]
    return sh, sc, ga


def _ada_kernel(c_ref, w_ref, b_ref, o_ref):
    ca = _silu(c_ref[...]).astype(BF16)
    o_ref[...] = _dot(ca, w_ref[...].astype(BF16)) + b_ref[...]


def _ada(c, w_ada, b_ada):
    bsz, d = c.shape
    n = w_ada.shape[1]
    tn = 1024
    c8 = jnp.zeros((8, d), F32).at[:bsz].set(c)
    out = pl.pallas_call(
        _ada_kernel,
        out_shape=jax.ShapeDtypeStruct((8, n), F32),
        grid=(n // tn,),
        in_specs=[pl.BlockSpec((8, d), lambda j: (0, 0)),
                  pl.BlockSpec((d, tn), lambda j: (0, j)),
                  pl.BlockSpec((1, tn), lambda j: (0, j))],
        out_specs=pl.BlockSpec((8, tn), lambda j: (0, j)),
        compiler_params=pltpu.CompilerParams(
            dimension_semantics=("arbitrary",), vmem_limit_bytes=VMEM_LIMIT_BYTES),
        name="ada_mod",
    )(c8, w_ada, b_ada.reshape(1, n))
    return out[:bsz]


def _ffn_kernel(x_ref, mod_ref, g_ref, win_ref, wout_ref, *rest, k, n_split, final):
    if final:
        fg_ref, o_ref = rest
    else:
        (o_ref,) = rest
    d = x_ref.shape[-1]
    f = wout_ref.shape[0]
    fc = f // n_split
    x = x_ref[...]
    sh, sc, ga = _mod_slices(mod_ref, k, d)
    h = _rms(x, g_ref[...]) * (1.0 + sc) + sh
    hb = h.astype(BF16)
    acc = None
    for j in range(n_split):
        a = _dot(hb, win_ref[:, j * fc:(j + 1) * fc])
        b = _dot(hb, win_ref[:, f + j * fc:f + (j + 1) * fc])
        act = (_silu(a) * b).astype(BF16)
        part = _dot(act, wout_ref[j * fc:(j + 1) * fc, :])
        acc = part if acc is None else acc + part
    y = x + (0.5 * ga) * acc
    if final:
        y = _rms(y, fg_ref[...])
    o_ref[...] = y


def _const_spec(shape):
    nd = len(shape)
    return pl.BlockSpec(shape, lambda *_: (0,) * nd, pipeline_mode=pl.Buffered(1))


def _ffn(x, mod3, norm_g, w_in, w_out, final_g, *, k, tm):
    bsz, seq, d = x.shape
    f = w_out.shape[0]
    final = final_g is not None
    tok = pl.BlockSpec((None, tm, d), lambda b, i: (b, i, 0))
    in_specs = [tok,
                pl.BlockSpec((None, 1, N_MOD * d), lambda b, i: (b, 0, 0)),
                _const_spec((1, d)),
                _const_spec((d, 2 * f)),
                _const_spec((f, d))]
    args = [x, mod3, norm_g.reshape(1, d), w_in, w_out]
    if final:
        in_specs.append(_const_spec((1, d)))
        args.append(final_g.reshape(1, d))
    return pl.pallas_call(
        functools.partial(_ffn_kernel, k=k, n_split=2, final=final),
        out_shape=jax.ShapeDtypeStruct((bsz, seq, d), F32),
        grid=(bsz, seq // tm),
        in_specs=in_specs,
        out_specs=tok,
        compiler_params=pltpu.CompilerParams(
            dimension_semantics=("parallel", "parallel"),
            vmem_limit_bytes=VMEM_LIMIT_BYTES),
        name="ffn_final" if final else "ffn",
    )(*args)


def _mix_in_kernel(x_ref, mod_ref, g2_ref, wuv_ref, wqkv_ref, wz_ref, wab_ref, wgate_ref,
                   convw_ref, alog_ref, dtb_ref, lng_ref, lnb_ref, ws_ref, bsf_ref,
                   oa_ref, qkv_ref, sz_ref, gate_ref, gb_ref, cbuf_ref):
    tm, d = x_ref.shape
    width = oa_ref.shape[-1]
    dnw = sz_ref.shape[-1]

    @pl.when(pl.program_id(1) == 0)
    def _():
        cbuf_ref[...] = jnp.zeros(cbuf_ref.shape, F32)

    x = x_ref[...]
    sh, sc, _ = _mod_slices(mod_ref, 1, d)
    hb = (_rms(x, g2_ref[...]) * (1.0 + sc) + sh).astype(BF16)

    uv = _dot(hb, wuv_ref[...])
    u = _gelu_exact(uv[:, :width])
    v = _gelu_exact(uv[:, width:])
    mu = jnp.mean(v, axis=-1, keepdims=True)
    vc = v - mu
    var = jnp.mean(vc * vc, axis=-1, keepdims=True)
    vb = (vc * lax.rsqrt(var + EPS) * lng_ref[...] + lnb_ref[...]).astype(BF16)
    row = lax.broadcasted_iota(jnp.int32, (GM_CHUNK, GM_CHUNK), 0)
    col = lax.broadcasted_iota(jnp.int32, (GM_CHUNK, GM_CHUNK), 1)
    tril = row >= col
    gd = width // GM_GROUPS
    for g in range(GM_GROUPS):
        wm = jnp.where(tril, ws_ref[g], 0.0).astype(BF16)
        bias = bsf_ref[:, g * gd:(g + 1) * gd]
        for c in range(tm // GM_CHUNK):
            rs = slice(c * GM_CHUNK, (c + 1) * GM_CHUNK)
            cs = slice(g * gd, (g + 1) * gd)
            mixed = _dot(wm, vb[rs, cs]) + bias
            oa_ref[rs, cs] = (u[rs, cs] * mixed).astype(oa_ref.dtype)

    pre = _dot(hb, wqkv_ref[...])
    carry = cbuf_ref[...]
    w_last = convw_ref[DN_CONV - 1:DN_CONV, :]
    conv = pre * w_last
    head = pre[:CONV_HALO] * w_last
    sub = lax.broadcasted_iota(jnp.int32, carry.shape, 0)
    for s in range(1, DN_CONV):
        w_s = convw_ref[DN_CONV - 1 - s:DN_CONV - s, :]
        rolled = pltpu.roll(pre, s, 0)
        conv = conv + rolled * w_s
        head = head + jnp.where(sub < s, pltpu.roll(carry, s, 0), rolled[:CONV_HALO]) * w_s
    cbuf_ref[...] = pre[tm - CONV_HALO:tm, :]
    act = _silu(jnp.concatenate([head, conv[CONV_HALO:]], axis=0))
    for h in range(DN_HEADS):
        for part, scale in ((0, DN_HEAD_DIM ** -0.5), (1, 1.0)):
            cs = slice(part * dnw + h * DN_HEAD_DIM, part * dnw + (h + 1) * DN_HEAD_DIM)
            t = act[:, cs]
            t = t * lax.rsqrt(jnp.sum(t * t, axis=-1, keepdims=True) + EPS)
            if part == 0:
                t = t * scale
            qkv_ref[:, cs] = t.astype(qkv_ref.dtype)
    qkv_ref[:, 2 * dnw:] = act[:, 2 * dnw:].astype(qkv_ref.dtype)

    sz_ref[...] = _silu(_dot(hb, wz_ref[...])).astype(sz_ref.dtype)
    gate_ref[...] = _sigmoid(_dot(hb, wgate_ref[...])).astype(gate_ref.dtype)
    ab = _dot(hb, wab_ref[...])
    gb_ref[:, :LANES] = -jnp.exp(alog_ref[...]) * jax.nn.softplus(ab[:, :LANES] + dtb_ref[...])
    gb_ref[:, LANES:] = _sigmoid(ab[:, LANES:])


def _mix_in(x, mod3, norm_g, wuv, wqkv, wz, wab, wgate, conv_w, alog, dtb, lng, lnb,
            w_s, bsf, *, tm):
    bsz, seq, d = x.shape
    width = wuv.shape[1] // 2
    dnw = wz.shape[1]

    def tok(n):
        return pl.BlockSpec((None, tm, n), lambda b, i: (b, i, 0))

    in_specs = [tok(d),
                pl.BlockSpec((None, 1, N_MOD * d), lambda b, i: (b, 0, 0)),
                _const_spec((1, d)),
                _const_spec(wuv.shape), _const_spec(wqkv.shape), _const_spec(wz.shape),
                _const_spec(wab.shape), _const_spec(wgate.shape),
                _const_spec(conv_w.shape), _const_spec(alog.shape), _const_spec(dtb.shape),
                _const_spec(lng.shape), _const_spec(lnb.shape),
                _const_spec(w_s.shape), _const_spec(bsf.shape)]
    out_shape = (jax.ShapeDtypeStruct((bsz, seq, width), BF16),
                 jax.ShapeDtypeStruct((bsz, seq, 3 * dnw), BF16),
                 jax.ShapeDtypeStruct((bsz, seq, dnw), BF16),
                 jax.ShapeDtypeStruct((bsz, seq, 2 * d), BF16),
                 jax.ShapeDtypeStruct((bsz, seq, 2 * LANES), F32))
    out_specs = (tok(width), tok(3 * dnw), tok(dnw), tok(2 * d), tok(2 * LANES))
    return pl.pallas_call(
        _mix_in_kernel,
        out_shape=out_shape,
        grid=(bsz, seq // tm),
        in_specs=in_specs,
        out_specs=out_specs,
        scratch_shapes=[pltpu.VMEM((CONV_HALO, 3 * dnw), F32)],
        compiler_params=pltpu.CompilerParams(
            dimension_semantics=("arbitrary", "arbitrary"),
            vmem_limit_bytes=VMEM_LIMIT_BYTES),
        name="mix_in",
    )(x, mod3, norm_g.reshape(1, d), wuv, wqkv, wz, wab, wgate, conv_w, alog, dtb,
      lng, lnb, w_s, bsf)


def _dn_out_kernel(qkv_ref, gb_ref, oa_ref, sz_ref, gate_ref, x_ref, mod_ref, ng_ref,
                   wb0_ref, wb1_ref, wo_ref, o_ref, s_ref, ob_ref):
    bsz, tm, d = x_ref.shape
    dnw = sz_ref.shape[-1]
    hd = DN_HEAD_DIM
    n_chunks = tm // DN_CHUNK
    chains = [(b, h) for b in range(bsz) for h in range(DN_HEADS)]

    @pl.when(pl.program_id(0) == 0)
    def _():
        s_ref[...] = jnp.zeros(s_ref.shape, F32)

    row = lax.broadcasted_iota(jnp.int32, (tm, tm), 0)
    col = lax.broadcasted_iota(jnp.int32, (tm, tm), 1)
    shift = DN_CHUNK.bit_length() - 1
    same = jnp.right_shift(row, shift) == jnp.right_shift(col, shift)
    causal = jnp.logical_and(same, col <= row)
    strict = jnp.logical_and(same, col < row)
    cum_sel = jnp.where(causal, 1.0, 0.0).astype(BF16)
    e_row = lax.broadcasted_iota(jnp.int32, (LANES, dnw), 0)
    e_col = lax.broadcasted_iota(jnp.int32, (LANES, dnw), 1)
    expand = jnp.where(jnp.right_shift(e_col, hd.bit_length() - 1) == e_row,
                       1.0, 0.0).astype(BF16)

    g_full, b_full, gc_t, eg_full = [], [], [], []
    for b in range(bsz):
        gc = _dot_sel_left(cum_sel, gb_ref[b, :, :LANES])
        g_full.append(_dot_sel_right(gc, expand))
        b_full.append(_dot_sel_right(gb_ref[b, :, LANES:], expand))
        gc_t.append(gc.T)
        eg_full.append(jnp.exp(g_full[b]))

    k_b, kf, gh, egh, decay, rhs, qd_b = {}, {}, {}, {}, {}, {}, {}
    kk, qk_b = {}, {}
    for c in chains:
        b, h = c
        cs = slice(h * hd, (h + 1) * hd)
        q = qkv_ref[b, :, h * hd:(h + 1) * hd]
        k_b[c] = qkv_ref[b, :, dnw + h * hd:dnw + (h + 1) * hd]
        v = qkv_ref[b, :, 2 * dnw + h * hd:2 * dnw + (h + 1) * hd]
        kf[c] = k_b[c].astype(F32)
        gh[c] = g_full[b][:, cs]
        egh[c] = eg_full[b][:, cs]
        bh = b_full[b][:, cs]
        kbeta = kf[c] * bh
        diff = gh[c] - gc_t[b][h:h + 1, :]
        decay[c] = jnp.where(causal, jnp.exp(jnp.where(causal, diff, 0.0)), 0.0)
        rhs[c] = jnp.concatenate([v.astype(F32) * bh, kbeta * egh[c]], axis=1)
        qd_b[c] = (q.astype(F32) * egh[c]).astype(BF16)
        kk[c] = _dot_nt(kbeta.astype(BF16), k_b[c])
        qk_b[c] = (_dot_nt(q, k_b[c]) * decay[c]).astype(BF16)

    n, x = {}, {}
    for c in chains:
        a_low = jnp.where(strict, kk[c] * decay[c], 0.0)
        a_b = a_low.astype(BF16)
        n[c] = -a_low
        x[c] = _dot(a_b, a_b)
    p = 2
    while p < DN_CHUNK:
        for c in chains:
            x_b = x[c].astype(BF16)
            n_b = n[c].astype(BF16)
            if 2 * p < DN_CHUNK:
                both = _dot(x_b, jnp.concatenate([x_b, n_b], axis=1))
                x_next, xn = both[:, :tm], both[:, tm:]
            else:
                x_next, xn = None, _dot(x_b, n_b)
            n[c] = n[c] + x[c] + xn
            x[c] = x_next
        p *= 2

    u, wq_b = {}, {}
    for c in chains:
        sol = rhs[c] + _dot(n[c].astype(BF16), rhs[c].astype(BF16))
        u[c] = sol[:, :hd]
        w_b = sol[:, hd:].astype(BF16)
        wq_b[c] = [jnp.concatenate([w_b[i * DN_CHUNK:(i + 1) * DN_CHUNK],
                                    qd_b[c][i * DN_CHUNK:(i + 1) * DN_CHUNK]], axis=0)
                   for i in range(n_chunks)]

    state = {c: s_ref[c[0] * DN_HEADS + c[1]] for c in chains}
    o_inter = {c: [] for c in chains}
    v_new = {c: [] for c in chains}
    for i in range(n_chunks):
        rs = slice(i * DN_CHUNK, (i + 1) * DN_CHUNK)
        last = (i + 1) * DN_CHUNK - 1
        for c in chains:
            both = _dot(wq_b[c][i], state[c].astype(BF16))
            vn_b = (u[c][rs] - both[:DN_CHUNK]).astype(BF16)
            o_inter[c].append(both[DN_CHUNK:])
            kd_b = (kf[c][rs] * jnp.exp(gh[c][last:last + 1, :] - gh[c][rs])).astype(BF16)
            state[c] = state[c] * egh[c][last:last + 1, :] + _dot_tn(kd_b, vn_b)
            v_new[c].append(vn_b)
    for c in chains:
        b, h = c
        cs = slice(h * hd, (h + 1) * hd)
        s_ref[b * DN_HEADS + h] = state[c]
        o_h = (jnp.concatenate(o_inter[c], axis=0)
               + _dot(qk_b[c], jnp.concatenate(v_new[c], axis=0)))
        o_h = o_h * lax.rsqrt(jnp.mean(o_h * o_h, axis=-1, keepdims=True) + EPS) * ng_ref[...]
        ob_ref[b * tm:(b + 1) * tm, cs] = (o_h * sz_ref[b, :, cs].astype(F32)).astype(ob_ref.dtype)

    oa = jnp.concatenate([oa_ref[b] for b in range(bsz)], axis=0)
    gate = jnp.concatenate([gate_ref[b] for b in range(bsz)], axis=0).astype(F32)
    merged = gate[:, :d] * _dot(oa, wb0_ref[...]) + gate[:, d:] * _dot(ob_ref[...], wb1_ref[...])
    y = _dot(merged.astype(BF16), wo_ref[...])
    for b in range(bsz):
        ga = mod_ref[b, :, 5 * d:6 * d]
        o_ref[b] = x_ref[b] + ga * y[b * tm:(b + 1) * tm]


def _dn_out(qkv, gb, oa, sz, gate, x, mod3, ng, wb0, wb1, wo, *, tm):
    bsz, seq, d = x.shape
    dnw = sz.shape[-1]

    def tok(n):
        return pl.BlockSpec((bsz, tm, n), lambda i: (0, i, 0))

    in_specs = [tok(3 * dnw), tok(2 * LANES), tok(oa.shape[-1]), tok(dnw), tok(2 * d), tok(d),
                _const_spec(mod3.shape),
                _const_spec(ng.shape), _const_spec(wb0.shape), _const_spec(wb1.shape),
                _const_spec(wo.shape)]
    return pl.pallas_call(
        _dn_out_kernel,
        out_shape=jax.ShapeDtypeStruct((bsz, seq, d), F32),
        grid=(seq // tm,),
        in_specs=in_specs,
        out_specs=tok(d),
        scratch_shapes=[pltpu.VMEM((bsz * DN_HEADS, DN_HEAD_DIM, DN_HEAD_DIM), F32),
                        pltpu.VMEM((bsz * tm, dnw), BF16)],
        compiler_params=pltpu.CompilerParams(
            dimension_semantics=("arbitrary",),
            vmem_limit_bytes=VMEM_LIMIT_BYTES),
        name="dn_out",
    )(qkv, gb, oa, sz, gate, x, mod3, ng, wb0, wb1, wo)


def _pad_lanes(w):
    pad = LANES - w.shape[-1]
    return jnp.pad(w, [(0, 0)] * (w.ndim - 1) + [(0, pad)])


def kernel(x, c, w_ada, b_ada, norm1_g, ffn1_w_in, ffn1_w_out, norm2_g, w_in, conv_w, a_log,
           dt_bias, dn_norm_g, gm_ln_g, gm_ln_b, gm_w_s, gm_b_s, w_branch, w_out, norm3_g,
           ffn2_w_in, ffn2_w_out, final_g):
    bsz, seq, d = x.shape
    depth = w_ada.shape[0]
    width = gm_ln_g.shape[-1]
    dnw = DN_HEADS * DN_HEAD_DIM
    tm_ffn = min(512, seq)
    tm_mix = min(256, seq)
    tm_dn = min(128, seq)

    for l in range(depth):
        mod3 = _ada(c, w_ada[l], b_ada[l]).reshape(bsz, 1, N_MOD * d)
        x = _ffn(x, mod3, norm1_g[l], ffn1_w_in[l].astype(BF16), ffn1_w_out[l].astype(BF16),
                 None, k=0, tm=tm_ffn)

        wl = w_in[l]
        o = 0
        wuv = wl[:, o:o + 2 * width].astype(BF16); o += 2 * width
        wqkv = wl[:, o:o + 3 * dnw].astype(BF16); o += 3 * dnw
        wz = wl[:, o:o + dnw].astype(BF16); o += dnw
        wa = wl[:, o:o + DN_HEADS]; o += DN_HEADS
        wb = wl[:, o:o + DN_HEADS]; o += DN_HEADS
        wgate = wl[:, o:].astype(BF16)
        wab = jnp.concatenate([_pad_lanes(wa), _pad_lanes(wb)], axis=1).astype(BF16)
        bsf = jnp.repeat(gm_b_s[l].T, width // GM_GROUPS, axis=1)
        oa, qkv, sz, gate, gb = _mix_in(
            x, mod3, norm2_g[l], wuv, wqkv, wz, wab, wgate, conv_w[l],
            _pad_lanes(a_log[l].reshape(1, -1)), _pad_lanes(dt_bias[l].reshape(1, -1)),
            gm_ln_g[l].reshape(1, -1), gm_ln_b[l].reshape(1, -1), gm_w_s[l], bsf, tm=tm_mix)
        x = _dn_out(qkv, gb, oa, sz, gate, x, mod3, dn_norm_g[l].reshape(1, -1),
                    w_branch[l, 0].astype(BF16), w_branch[l, 1].astype(BF16),
                    w_out[l].astype(BF16), tm=tm_dn)

        last = l == depth - 1
        x = _ffn(x, mod3, norm3_g[l], ffn2_w_in[l].astype(BF16), ffn2_w_out[l].astype(BF16),
                 final_g if last else None, k=2, tm=tm_ffn)
    if depth == 0:
        raise ValueError("depth must be >= 1")
    return x
```

```python
import functools

import jax
import jax.numpy as jnp
from jax import lax
from jax.experimental import pallas as pl
from jax.experimental.pallas import tpu as pltpu

F32 = jnp.float32
BF16 = jnp.bfloat16

EPS = 1e-6
N_MOD = 9
LANES = 128
GM_CHUNK = 128
GM_GROUPS = 8
DN_HEADS = 8
DN_HEAD_DIM = 128
DN_CONV = 4
DN_CHUNK = 64
CONV_HALO = 8
VMEM_LIMIT_BYTES = 56 * 1024 * 1024


def _dot(a, b):
    return jnp.dot(a, b, preferred_element_type=F32)


def _dot_nt(a, b):
    return lax.dot_general(a, b, (((1,), (1,)), ((), ())), preferred_element_type=F32)


def _dot_tn(a, b):
    return lax.dot_general(a, b, (((0,), (0,)), ((), ())), preferred_element_type=F32)


def _split3(x):
    hi = x.astype(BF16)
    r1 = x - hi.astype(F32)
    mid = r1.astype(BF16)
    lo = (r1 - mid.astype(F32)).astype(BF16)
    return hi, mid, lo


def _dot_sel_left(sel, x):
    hi, mid, lo = _split3(x)
    return _dot(sel, hi) + _dot(sel, mid) + _dot(sel, lo)


def _sigmoid(x):
    return jax.nn.sigmoid(x)


def _silu(x):
    return x * _sigmoid(x)


def _gelu_exact(x):
    return 0.5 * x * (1.0 + lax.erf(x * (2.0 ** -0.5)))


def _rms(x, g):
    return x * lax.rsqrt(jnp.mean(x * x, axis=-1, keepdims=True) + EPS) * g


def _mod_slices(mod_ref, k, d):
    sh = mod_ref[:, (3 * k) * d:(3 * k + 1) * d]
    sc = mod_ref[:, (3 * k + 1) * d:(3 * k + 2) * d]
    ga = mod_ref[:, (3 * k + 2) * d:(3 * k + 3) * d]
    return sh, sc, ga


def _ada_kernel(c_ref, w_ref, b_ref, o_ref):
    ca = _silu(c_ref[...]).astype(BF16)
    o_ref[...] = _dot(ca, w_ref[...].astype(BF16)) + b_ref[...]


def _ada(c, w_ada, b_ada, layer):
    bsz, d = c.shape
    n = w_ada.shape[-1]
    tn = 1024
    c8 = jnp.zeros((8, d), F32).at[:bsz].set(c)
    out = pl.pallas_call(
        _ada_kernel,
        out_shape=jax.ShapeDtypeStruct((8, n), F32),
        grid=(n // tn,),
        in_specs=[pl.BlockSpec((8, d), lambda j: (0, 0)),
                  pl.BlockSpec((None, d, tn), lambda j: (layer, 0, j)),
                  pl.BlockSpec((None, 1, tn), lambda j: (layer, 0, j))],
        out_specs=pl.BlockSpec((8, tn), lambda j: (0, j)),
        compiler_params=pltpu.CompilerParams(
            dimension_semantics=("arbitrary",), vmem_limit_bytes=VMEM_LIMIT_BYTES),
        name="ada_mod",
    )(c8, w_ada, b_ada)
    return out[:bsz]


def _modulated_norm(x, g, mod_ref, k):
    sh, sc, _ = _mod_slices(mod_ref, k, x.shape[-1])
    return (_rms(x, g) * (1.0 + sc) + sh).astype(BF16)


def _ffn_kernel(x_ref, mod_ref, g_ref, win_ref, wout_ref, post_g_ref, *out_refs, k, post_k):
    d = x_ref.shape[-1]
    f = wout_ref.shape[0]
    x = x_ref[...]
    hb = _modulated_norm(x, g_ref[...], mod_ref, k)
    a = _dot(hb, win_ref[:, :f])
    b = _dot(hb, win_ref[:, f:])
    act = (_silu(a) * b).astype(BF16)
    ga = mod_ref[:, (3 * k + 2) * d:(3 * k + 3) * d]
    y = x + (0.5 * ga) * _dot(act, wout_ref[...])
    if post_k is None:
        out_refs[0][...] = _rms(y, post_g_ref[...])
    else:
        out_refs[0][...] = y
        out_refs[1][...] = _modulated_norm(y, post_g_ref[...], mod_ref, post_k)


def _const_spec(shape):
    nd = len(shape)
    return pl.BlockSpec(shape, lambda *_: (0,) * nd, pipeline_mode=pl.Buffered(1))


def _ffn(x, mod3, norm_g, w_in, w_out, post_g, *, k, post_k, tm):
    bsz, seq, d = x.shape
    f = w_out.shape[0]
    tok = pl.BlockSpec((None, tm, d), lambda b, i: (b, i, 0))
    in_specs = [tok,
                pl.BlockSpec((None, 1, N_MOD * d), lambda b, i: (b, 0, 0)),
                _const_spec((1, d)),
                _const_spec((d, 2 * f)),
                _const_spec((f, d)),
                _const_spec((1, d))]
    x_out = jax.ShapeDtypeStruct((bsz, seq, d), F32)
    if post_k is None:
        out_shape, out_specs = x_out, tok
    else:
        out_shape = (x_out, jax.ShapeDtypeStruct((bsz, seq, d), BF16))
        out_specs = (tok, tok)
    return pl.pallas_call(
        functools.partial(_ffn_kernel, k=k, post_k=post_k),
        out_shape=out_shape,
        grid=(bsz, seq // tm),
        in_specs=in_specs,
        out_specs=out_specs,
        compiler_params=pltpu.CompilerParams(
            dimension_semantics=("parallel", "parallel"),
            vmem_limit_bytes=VMEM_LIMIT_BYTES),
        name="ffn_final" if post_k is None else "ffn",
    )(x, mod3, norm_g.reshape(1, d), w_in, w_out, post_g.reshape(1, d))


def _causal_conv(pre, carry, convw_ref):
    w_taps = [convw_ref[DN_CONV - 1 - s:DN_CONV - s, :] for s in range(DN_CONV)]
    groups = [pre[r:r + CONV_HALO] for r in range(0, pre.shape[0], CONV_HALO)]
    out = [g * w_taps[0] for g in groups]
    sub = lax.broadcasted_iota(jnp.int32, carry.shape, 0)
    for s in range(1, DN_CONV):
        rot_prev = pltpu.roll(carry, s, 0)
        for r, g in enumerate(groups):
            rot = pltpu.roll(g, s, 0)
            out[r] = out[r] + jnp.where(sub < s, rot_prev, rot) * w_taps[s]
            rot_prev = rot
    return jnp.concatenate(out, axis=0)


def _mix_in_kernel(h_ref, wuv_ref, wqkv_ref, wz_ref, wab_ref, wgate_ref,
                   convw_ref, alog_ref, dtb_ref, lng_ref, lnb_ref, ws_ref, bsf_ref,
                   oa_ref, qkv_ref, sz_ref, gate_ref, gb_ref, cbuf_ref, *, n_sub):
    tm = h_ref.shape[0]
    width = oa_ref.shape[-1]
    dnw = sz_ref.shape[-1]
    ts = tm // n_sub
    gd = width // GM_GROUPS

    @pl.when(pl.program_id(1) == 0)
    def _():
        cbuf_ref[...] = jnp.zeros(cbuf_ref.shape, F32)

    row = lax.broadcasted_iota(jnp.int32, (GM_CHUNK, GM_CHUNK), 0)
    col = lax.broadcasted_iota(jnp.int32, (GM_CHUNK, GM_CHUNK), 1)
    tril = row >= col
    w_mix = [jnp.where(tril, ws_ref[g], 0.0).astype(BF16) for g in range(GM_GROUPS)]
    carry = cbuf_ref[...]

    for r0 in range(0, tm, ts):
        rows = slice(r0, r0 + ts)
        hb = h_ref[rows, :]

        uv = _dot(hb, wuv_ref[...])
        u = _gelu_exact(uv[:, :width])
        v = _gelu_exact(uv[:, width:])
        mu = jnp.mean(v, axis=-1, keepdims=True)
        vc = v - mu
        var = jnp.mean(vc * vc, axis=-1, keepdims=True)
        vb = (vc * lax.rsqrt(var + EPS) * lng_ref[...] + lnb_ref[...]).astype(BF16)
        for g in range(GM_GROUPS):
            cs = slice(g * gd, (g + 1) * gd)
            for c0 in range(0, ts, GM_CHUNK):
                mixed = _dot(w_mix[g], vb[c0:c0 + GM_CHUNK, cs]) + bsf_ref[:, cs]
                oa_ref[r0 + c0:r0 + c0 + GM_CHUNK, cs] = (
                    u[c0:c0 + GM_CHUNK, cs] * mixed).astype(oa_ref.dtype)

        pre = _dot(hb, wqkv_ref[...])
        act = _silu(_causal_conv(pre, carry, convw_ref))
        carry = pre[ts - CONV_HALO:ts, :]
        for h in range(DN_HEADS):
            for part in range(2):
                cs = slice(part * dnw + h * DN_HEAD_DIM, part * dnw + (h + 1) * DN_HEAD_DIM)
                t = act[:, cs]
                t = t * lax.rsqrt(jnp.sum(t * t, axis=-1, keepdims=True) + EPS)
                if part == 0:
                    t = t * (DN_HEAD_DIM ** -0.5)
                qkv_ref[rows, cs] = t.astype(qkv_ref.dtype)
        qkv_ref[rows, 2 * dnw:] = act[:, 2 * dnw:].astype(qkv_ref.dtype)

        sz_ref[rows, :] = _silu(_dot(hb, wz_ref[...])).astype(sz_ref.dtype)
        gate_ref[rows, :] = _sigmoid(_dot(hb, wgate_ref[...])).astype(gate_ref.dtype)
        ab = _dot(hb, wab_ref[...])
        gb_ref[rows, :LANES] = (-jnp.exp(alog_ref[...])
                                * jax.nn.softplus(ab[:, :LANES] + dtb_ref[...]))
        gb_ref[rows, LANES:] = _sigmoid(ab[:, LANES:])
    cbuf_ref[...] = carry


def _mix_in(h, wuv, wqkv, wz, wab, wgate, conv_w, alog, dtb, lng, lnb, w_s, bsf, *, tm):
    bsz, seq, d = h.shape
    width = wuv.shape[1] // 2
    dnw = wz.shape[1]

    def tok(n):
        return pl.BlockSpec((None, tm, n), lambda b, i: (b, i, 0))

    in_specs = [tok(d),
                _const_spec(wuv.shape), _const_spec(wqkv.shape), _const_spec(wz.shape),
                _const_spec(wab.shape), _const_spec(wgate.shape),
                _const_spec(conv_w.shape), _const_spec(alog.shape), _const_spec(dtb.shape),
                _const_spec(lng.shape), _const_spec(lnb.shape),
                _const_spec(w_s.shape), _const_spec(bsf.shape)]
    out_shape = (jax.ShapeDtypeStruct((bsz, seq, width), BF16),
                 jax.ShapeDtypeStruct((bsz, seq, 3 * dnw), BF16),
                 jax.ShapeDtypeStruct((bsz, seq, dnw), BF16),
                 jax.ShapeDtypeStruct((bsz, seq, 2 * d), BF16),
                 jax.ShapeDtypeStruct((bsz, seq, 2 * LANES), F32))
    out_specs = (tok(width), tok(3 * dnw), tok(dnw), tok(2 * d), tok(2 * LANES))
    return pl.pallas_call(
        functools.partial(_mix_in_kernel, n_sub=2),
        out_shape=out_shape,
        grid=(bsz, seq // tm),
        in_specs=in_specs,
        out_specs=out_specs,
        scratch_shapes=[pltpu.VMEM((CONV_HALO, 3 * dnw), F32)],
        compiler_params=pltpu.CompilerParams(
            dimension_semantics=("arbitrary", "arbitrary"),
            vmem_limit_bytes=VMEM_LIMIT_BYTES),
        name="mix_in",
    )(h, wuv, wqkv, wz, wab, wgate, conv_w, alog, dtb, lng, lnb, w_s, bsf)


def _dn_out_kernel(qkv_ref, gb_ref, oa_ref, sz_ref, gate_ref, x_ref, mod_ref, ng_ref,
                   wb0_ref, wb1_ref, wo_ref, o_ref, s_ref, ob_ref):
    bsz, tm, d = x_ref.shape
    dnw = sz_ref.shape[-1]
    hd = DN_HEAD_DIM
    n_chunks = tm // DN_CHUNK
    chains = [(b, h) for b in range(bsz) for h in range(DN_HEADS)]

    @pl.when(pl.program_id(0) == 0)
    def _():
        s_ref[...] = jnp.zeros(s_ref.shape, F32)

    row = lax.broadcasted_iota(jnp.int32, (tm, tm), 0)
    col = lax.broadcasted_iota(jnp.int32, (tm, tm), 1)
    shift = DN_CHUNK.bit_length() - 1
    same = jnp.right_shift(row, shift) == jnp.right_shift(col, shift)
    causal = jnp.logical_and(same, col <= row)
    strict = jnp.logical_and(same, col < row)
    cum_sel = jnp.where(causal, 1.0, 0.0).astype(BF16)
    e_row = lax.broadcasted_iota(jnp.int32, (2 * LANES, dnw), 0)
    e_col = lax.broadcasted_iota(jnp.int32, (2 * LANES, dnw), 1)
    expand = jnp.where(jnp.right_shift(e_col, hd.bit_length() - 1)
                       == jnp.bitwise_and(e_row, LANES - 1), 1.0, 0.0).astype(BF16)

    def lane_expand(x):
        hi = x.astype(BF16)
        mid = (x - hi.astype(F32)).astype(BF16)
        return _dot(jnp.concatenate([hi, mid], axis=1), expand)

    g_full, b_full, gc_t, eg_full = [], [], [], []
    for b in range(bsz):
        gc = _dot_sel_left(cum_sel, gb_ref[b, :, :LANES])
        g_full.append(lane_expand(gc))
        b_full.append(lane_expand(gb_ref[b, :, LANES:]))
        gc_t.append(gc.T)
        eg_full.append(jnp.exp(g_full[b]))

    k_b, kf, gh, egh, decay, rhs, qd_b = {}, {}, {}, {}, {}, {}, {}
    kk, qk_b = {}, {}
    for c in chains:
        b, h = c
        cs = slice(h * hd, (h + 1) * hd)
        q = qkv_ref[b, :, h * hd:(h + 1) * hd]
        k_b[c] = qkv_ref[b, :, dnw + h * hd:dnw + (h + 1) * hd]
        v = qkv_ref[b, :, 2 * dnw + h * hd:2 * dnw + (h + 1) * hd]
        kf[c] = k_b[c].astype(F32)
        gh[c] = g_full[b][:, cs]
        egh[c] = eg_full[b][:, cs]
        bh = b_full[b][:, cs]
        kbeta = kf[c] * bh
        diff = gh[c] - gc_t[b][h:h + 1, :]
        decay[c] = jnp.where(causal, jnp.exp(jnp.where(causal, diff, 0.0)), 0.0)
        rhs[c] = jnp.concatenate([v.astype(F32) * bh, kbeta * egh[c]], axis=1)
        qd_b[c] = (q.astype(F32) * egh[c]).astype(BF16)
        kk[c] = _dot_nt(kbeta.astype(BF16), k_b[c])
        qk_b[c] = (_dot_nt(q, k_b[c]) * decay[c]).astype(BF16)

    n, x = {}, {}
    for c in chains:
        a_low = jnp.where(strict, kk[c] * decay[c], 0.0)
        a_b = a_low.astype(BF16)
        n[c] = -a_low
        x[c] = _dot(a_b, a_b)
    p = 2
    while p < DN_CHUNK:
        for c in chains:
            x_b = x[c].astype(BF16)
            n_b = n[c].astype(BF16)
            if 2 * p < DN_CHUNK:
                both = _dot(x_b, jnp.concatenate([x_b, n_b], axis=1))
                x_next, xn = both[:, :tm], both[:, tm:]
            else:
                x_next, xn = None, _dot(x_b, n_b)
            n[c] = n[c] + x[c] + xn
            x[c] = x_next
        p *= 2

    u, wq_b = {}, {}
    for c in chains:
        sol = rhs[c] + _dot(n[c].astype(BF16), rhs[c].astype(BF16))
        u[c] = sol[:, :hd]
        w_b = sol[:, hd:].astype(BF16)
        wq_b[c] = [jnp.concatenate([w_b[i * DN_CHUNK:(i + 1) * DN_CHUNK],
                                    qd_b[c][i * DN_CHUNK:(i + 1) * DN_CHUNK]], axis=0)
                   for i in range(n_chunks)]

    state = {c: s_ref[c[0] * DN_HEADS + c[1]] for c in chains}
    o_inter = {c: [] for c in chains}
    v_new = {c: [] for c in chains}
    for i in range(n_chunks):
        rs = slice(i * DN_CHUNK, (i + 1) * DN_CHUNK)
        last = (i + 1) * DN_CHUNK - 1
        for c in chains:
            both = _dot(wq_b[c][i], state[c].astype(BF16))
            vn_b = (u[c][rs] - both[:DN_CHUNK]).astype(BF16)
            o_inter[c].append(both[DN_CHUNK:])
            kd_b = (kf[c][rs] * jnp.exp(gh[c][last:last + 1, :] - gh[c][rs])).astype(BF16)
            state[c] = state[c] * egh[c][last:last + 1, :] + _dot_tn(kd_b, vn_b)
            v_new[c].append(vn_b)
    for c in chains:
        b, h = c
        cs = slice(h * hd, (h + 1) * hd)
        s_ref[b * DN_HEADS + h] = state[c]
        o_h = (jnp.concatenate(o_inter[c], axis=0)
               + _dot(qk_b[c], jnp.concatenate(v_new[c], axis=0)))
        o_h = o_h * lax.rsqrt(jnp.mean(o_h * o_h, axis=-1, keepdims=True) + EPS) * ng_ref[...]
        ob_ref[b * tm:(b + 1) * tm, cs] = (o_h * sz_ref[b, :, cs].astype(F32)).astype(ob_ref.dtype)

    oa = jnp.concatenate([oa_ref[b] for b in range(bsz)], axis=0)
    gate = jnp.concatenate([gate_ref[b] for b in range(bsz)], axis=0).astype(F32)
    merged = gate[:, :d] * _dot(oa, wb0_ref[...]) + gate[:, d:] * _dot(ob_ref[...], wb1_ref[...])
    y = _dot(merged.astype(BF16), wo_ref[...])
    for b in range(bsz):
        ga = mod_ref[b, :, 5 * d:6 * d]
        o_ref[b] = x_ref[b] + ga * y[b * tm:(b + 1) * tm]


def _dn_out(qkv, gb, oa, sz, gate, x, mod3, ng, wb0, wb1, wo, *, tm):
    bsz, seq, d = x.shape
    dnw = sz.shape[-1]

    def tok(n):
        return pl.BlockSpec((bsz, tm, n), lambda i: (0, i, 0))

    in_specs = [tok(3 * dnw), tok(2 * LANES), tok(oa.shape[-1]), tok(dnw), tok(2 * d), tok(d),
                _const_spec(mod3.shape),
                _const_spec(ng.shape), _const_spec(wb0.shape), _const_spec(wb1.shape),
                _const_spec(wo.shape)]
    return pl.pallas_call(
        _dn_out_kernel,
        out_shape=jax.ShapeDtypeStruct((bsz, seq, d), F32),
        grid=(seq // tm,),
        in_specs=in_specs,
        out_specs=tok(d),
        scratch_shapes=[pltpu.VMEM((bsz * DN_HEADS, DN_HEAD_DIM, DN_HEAD_DIM), F32),
                        pltpu.VMEM((bsz * tm, dnw), BF16)],
        compiler_params=pltpu.CompilerParams(
            dimension_semantics=("arbitrary",),
            vmem_limit_bytes=VMEM_LIMIT_BYTES),
        name="dn_out",
    )(qkv, gb, oa, sz, gate, x, mod3, ng, wb0, wb1, wo)


def _pad_lanes(w):
    pad = LANES - w.shape[-1]
    return jnp.pad(w, [(0, 0)] * (w.ndim - 1) + [(0, pad)])


def kernel(x, c, w_ada, b_ada, norm1_g, ffn1_w_in, ffn1_w_out, norm2_g, w_in, conv_w, a_log,
           dt_bias, dn_norm_g, gm_ln_g, gm_ln_b, gm_w_s, gm_b_s, w_branch, w_out, norm3_g,
           ffn2_w_in, ffn2_w_out, final_g):
    bsz, seq, d = x.shape
    depth = w_ada.shape[0]
    assert depth == 1, "the final rmsnorm is fused into the last layer's second FFN"
    width = gm_ln_g.shape[-1]
    dnw = DN_HEADS * DN_HEAD_DIM
    tm_ffn = min(512, seq)
    tm_mix = min(512, seq)
    tm_dn = min(128, seq)

    for l in range(depth):
        mod3 = _ada(c, w_ada, b_ada.reshape(depth, 1, N_MOD * d), l).reshape(bsz, 1, N_MOD * d)
        x, h = _ffn(x, mod3, norm1_g[l], ffn1_w_in[l].astype(BF16), ffn1_w_out[l].astype(BF16),
                    norm2_g[l], k=0, post_k=1, tm=tm_ffn)

        wl = w_in[l]
        o = 0
        wuv = wl[:, o:o + 2 * width].astype(BF16); o += 2 * width
        wqkv = wl[:, o:o + 3 * dnw].astype(BF16); o += 3 * dnw
        wz = wl[:, o:o + dnw].astype(BF16); o += dnw
        wa = wl[:, o:o + DN_HEADS]; o += DN_HEADS
        wb = wl[:, o:o + DN_HEADS]; o += DN_HEADS
        wgate = wl[:, o:].astype(BF16)
        wab = jnp.concatenate([_pad_lanes(wa), _pad_lanes(wb)], axis=1).astype(BF16)
        bsf = jnp.repeat(gm_b_s[l].T, width // GM_GROUPS, axis=1)
        oa, qkv, sz, gate, gb = _mix_in(
            h, wuv, wqkv, wz, wab, wgate, conv_w[l],
            _pad_lanes(a_log[l].reshape(1, -1)), _pad_lanes(dt_bias[l].reshape(1, -1)),
            gm_ln_g[l].reshape(1, -1), gm_ln_b[l].reshape(1, -1), gm_w_s[l], bsf, tm=tm_mix)
        x = _dn_out(qkv, gb, oa, sz, gate, x, mod3, dn_norm_g[l].reshape(1, -1),
                    w_branch[l, 0].astype(BF16), w_branch[l, 1].astype(BF16),
                    w_out[l].astype(BF16), tm=tm_dn)

        x = _ffn(x, mod3, norm3_g[l], ffn2_w_in[l].astype(BF16), ffn2_w_out[l].astype(BF16),
                 final_g, k=2, post_k=None, tm=tm_ffn)
    return x
```

```python
import functools

import jax
import jax.numpy as jnp
from jax import lax
from jax.experimental import pallas as pl
from jax.experimental.pallas import tpu as pltpu

F32 = jnp.float32
BF16 = jnp.bfloat16

EPS = 1e-6
N_MOD = 9
LANES = 128
GM_CHUNK = 128
GM_GROUPS = 8
DN_HEADS = 8
DN_HEAD_DIM = 128
DN_CONV = 4
DN_CHUNK = 64
CONV_HALO = 8
VMEM_LIMIT_BYTES = 56 * 1024 * 1024


def _dot(a, b):
    return jnp.dot(a, b, preferred_element_type=F32)


def _dot_nt(a, b):
    return lax.dot_general(a, b, (((1,), (1,)), ((), ())), preferred_element_type=F32)


def _dot_tn(a, b):
    return lax.dot_general(a, b, (((0,), (0,)), ((), ())), preferred_element_type=F32)


def _split3(x):
    hi = x.astype(BF16)
    r1 = x - hi.astype(F32)
    mid = r1.astype(BF16)
    lo = (r1 - mid.astype(F32)).astype(BF16)
    return hi, mid, lo


def _dot_sel_left(sel, x):
    hi, mid, lo = _split3(x)
    return _dot(sel, hi) + _dot(sel, mid) + _dot(sel, lo)


def _sigmoid(x):
    return jax.nn.sigmoid(x)


def _silu(x):
    return x * _sigmoid(x)


def _gelu_exact(x):
    return 0.5 * x * (1.0 + lax.erf(x * (2.0 ** -0.5)))


def _rms(x, g):
    return x * lax.rsqrt(jnp.mean(x * x, axis=-1, keepdims=True) + EPS) * g


def _mod_slices(mod_ref, k, d):
    sh = mod_ref[:, (3 * k) * d:(3 * k + 1) * d]
    sc = mod_ref[:, (3 * k + 1) * d:(3 * k + 2) * d]
    ga = mod_ref[:, (3 * k + 2) * d:(3 * k + 3) * d]
    return sh, sc, ga


def _ada_kernel(c_ref, w_ref, b_ref, o_ref):
    ca = _silu(c_ref[...]).astype(BF16)
    o_ref[...] = _dot(ca, w_ref[...].astype(BF16)) + b_ref[...]


def _ada(c, w_ada, b_ada, layer):
    bsz, d = c.shape
    n = w_ada.shape[-1]
    tn = 1024
    c8 = jnp.zeros((8, d), F32).at[:bsz].set(c)
    out = pl.pallas_call(
        _ada_kernel,
        out_shape=jax.ShapeDtypeStruct((8, n), F32),
        grid=(n // tn,),
        in_specs=[pl.BlockSpec((8, d), lambda j: (0, 0)),
                  pl.BlockSpec((None, d, tn), lambda j: (layer, 0, j)),
                  pl.BlockSpec((None, 1, tn), lambda j: (layer, 0, j))],
        out_specs=pl.BlockSpec((8, tn), lambda j: (0, j)),
        compiler_params=pltpu.CompilerParams(
            dimension_semantics=("arbitrary",), vmem_limit_bytes=VMEM_LIMIT_BYTES),
        name="ada_mod",
    )(c8, w_ada, b_ada)
    return out[:bsz]


def _modulated_norm(x, g, mod_ref, k):
    sh, sc, _ = _mod_slices(mod_ref, k, x.shape[-1])
    return (_rms(x, g) * (1.0 + sc) + sh).astype(BF16)


def _ffn_kernel(x_ref, mod_ref, g_ref, win_ref, wout_ref, post_g_ref, *out_refs, k, post_k):
    d = x_ref.shape[-1]
    f = wout_ref.shape[0]
    x = x_ref[...]
    hb = _modulated_norm(x, g_ref[...], mod_ref, k)
    a = _dot(hb, win_ref[:, :f])
    b = _dot(hb, win_ref[:, f:])
    act = (_silu(a) * b).astype(BF16)
    ga = mod_ref[:, (3 * k + 2) * d:(3 * k + 3) * d]
    y = x + (0.5 * ga) * _dot(act, wout_ref[...])
    if post_k is None:
        out_refs[0][...] = _rms(y, post_g_ref[...])
    else:
        out_refs[0][...] = y
        out_refs[1][...] = _modulated_norm(y, post_g_ref[...], mod_ref, post_k)


def _const_spec(shape):
    nd = len(shape)
    return pl.BlockSpec(shape, lambda *_: (0,) * nd, pipeline_mode=pl.Buffered(1))


def _ffn(x, mod3, norm_g, w_in, w_out, post_g, *, k, post_k, tm):
    bsz, seq, d = x.shape
    f = w_out.shape[0]
    tok = pl.BlockSpec((None, tm, d), lambda b, i: (b, i, 0))
    in_specs = [tok,
                pl.BlockSpec((None, 1, N_MOD * d), lambda b, i: (b, 0, 0)),
                _const_spec((1, d)),
                _const_spec((d, 2 * f)),
                _const_spec((f, d)),
                _const_spec((1, d))]
    x_out = jax.ShapeDtypeStruct((bsz, seq, d), F32)
    if post_k is None:
        out_shape, out_specs = x_out, tok
    else:
        out_shape = (x_out, jax.ShapeDtypeStruct((bsz, seq, d), BF16))
        out_specs = (tok, tok)
    return pl.pallas_call(
        functools.partial(_ffn_kernel, k=k, post_k=post_k),
        out_shape=out_shape,
        grid=(bsz, seq // tm),
        in_specs=in_specs,
        out_specs=out_specs,
        compiler_params=pltpu.CompilerParams(
            dimension_semantics=("parallel", "parallel"),
            vmem_limit_bytes=VMEM_LIMIT_BYTES),
        name="ffn_final" if post_k is None else "ffn",
    )(x, mod3, norm_g.reshape(1, d), w_in, w_out, post_g.reshape(1, d))


def _causal_conv(pre, carry, convw_ref):
    w_taps = [convw_ref[DN_CONV - 1 - s:DN_CONV - s, :] for s in range(DN_CONV)]
    groups = [pre[r:r + CONV_HALO] for r in range(0, pre.shape[0], CONV_HALO)]
    out = [g * w_taps[0] for g in groups]
    sub = lax.broadcasted_iota(jnp.int32, carry.shape, 0)
    for s in range(1, DN_CONV):
        rot_prev = pltpu.roll(carry, s, 0)
        for r, g in enumerate(groups):
            rot = pltpu.roll(g, s, 0)
            out[r] = out[r] + jnp.where(sub < s, rot_prev, rot) * w_taps[s]
            rot_prev = rot
    return jnp.concatenate(out, axis=0)


def _mix_in_kernel(h_ref, w_ref, convw_ref, alog_ref, dtb_ref, lng_ref, lnb_ref, ws_ref,
                   bsf_ref, oa_ref, qkv_ref, sz_ref, gate_ref, gb_ref, cbuf_ref, *, n_sub):
    tm = h_ref.shape[0]
    width = oa_ref.shape[-1]
    dnw = sz_ref.shape[-1]
    ts = tm // n_sub
    gd = width // GM_GROUPS
    o_qkv = 2 * width
    o_z = o_qkv + 3 * dnw
    o_ab = o_z + dnw
    o_gate = o_ab + 2 * LANES

    @pl.when(pl.program_id(1) == 0)
    def _():
        cbuf_ref[...] = jnp.zeros(cbuf_ref.shape, F32)

    row = lax.broadcasted_iota(jnp.int32, (GM_CHUNK, GM_CHUNK), 0)
    col = lax.broadcasted_iota(jnp.int32, (GM_CHUNK, GM_CHUNK), 1)
    tril = row >= col
    w_mix = [jnp.where(tril, ws_ref[g], 0.0).astype(BF16) for g in range(GM_GROUPS)]
    carry = cbuf_ref[...]

    for r0 in range(0, tm, ts):
        rows = slice(r0, r0 + ts)
        hb = h_ref[rows, :]

        uv = _dot(hb, w_ref[:, :o_qkv])
        pre = _dot(hb, w_ref[:, o_qkv:o_z])
        z = _dot(hb, w_ref[:, o_z:o_ab])
        ab = _dot(hb, w_ref[:, o_ab:o_gate])
        gate_logits = _dot(hb, w_ref[:, o_gate:])

        u = _gelu_exact(uv[:, :width])
        v = _gelu_exact(uv[:, width:])
        mu = jnp.mean(v, axis=-1, keepdims=True)
        vc = v - mu
        var = jnp.mean(vc * vc, axis=-1, keepdims=True)
        vb = (vc * lax.rsqrt(var + EPS) * lng_ref[...] + lnb_ref[...]).astype(BF16)
        for g in range(GM_GROUPS):
            cs = slice(g * gd, (g + 1) * gd)
            for c0 in range(0, ts, 2 * GM_CHUNK):
                pair = jnp.concatenate([vb[c0:c0 + GM_CHUNK, cs],
                                        vb[c0 + GM_CHUNK:c0 + 2 * GM_CHUNK, cs]], axis=1)
                mixed = _dot(w_mix[g], pair)
                for j in range(2):
                    cr = slice(c0 + j * GM_CHUNK, c0 + (j + 1) * GM_CHUNK)
                    mix_j = mixed[:, j * gd:(j + 1) * gd] + bsf_ref[:, cs]
                    oa_ref[r0 + cr.start:r0 + cr.stop, cs] = (
                        u[cr, cs] * mix_j).astype(oa_ref.dtype)

        act = _silu(_causal_conv(pre, carry, convw_ref))
        carry = pre[ts - CONV_HALO:ts, :]
        for h in range(DN_HEADS):
            for part in range(2):
                cs = slice(part * dnw + h * DN_HEAD_DIM, part * dnw + (h + 1) * DN_HEAD_DIM)
                t = act[:, cs]
                t = t * lax.rsqrt(jnp.sum(t * t, axis=-1, keepdims=True) + EPS)
                if part == 0:
                    t = t * (DN_HEAD_DIM ** -0.5)
                qkv_ref[rows, cs] = t.astype(qkv_ref.dtype)
        qkv_ref[rows, 2 * dnw:] = act[:, 2 * dnw:].astype(qkv_ref.dtype)

        sz_ref[rows, :] = _silu(z).astype(sz_ref.dtype)
        gate_ref[rows, :] = _sigmoid(gate_logits).astype(gate_ref.dtype)
        gb_ref[rows, :LANES] = (-jnp.exp(alog_ref[...])
                                * jax.nn.softplus(ab[:, :LANES] + dtb_ref[...]))
        gb_ref[rows, LANES:] = _sigmoid(ab[:, LANES:])
    cbuf_ref[...] = carry


def _mix_in(h, w_pack, conv_w, alog, dtb, lng, lnb, w_s, bsf, *, tm):
    bsz, seq, d = h.shape
    width = lng.shape[-1]
    dnw = conv_w.shape[-1] // 3

    def tok(n):
        return pl.BlockSpec((None, tm, n), lambda b, i: (b, i, 0))

    in_specs = [tok(d),
                _const_spec(w_pack.shape),
                _const_spec(conv_w.shape), _const_spec(alog.shape), _const_spec(dtb.shape),
                _const_spec(lng.shape), _const_spec(lnb.shape),
                _const_spec(w_s.shape), _const_spec(bsf.shape)]
    out_shape = (jax.ShapeDtypeStruct((bsz, seq, width), BF16),
                 jax.ShapeDtypeStruct((bsz, seq, 3 * dnw), BF16),
                 jax.ShapeDtypeStruct((bsz, seq, dnw), BF16),
                 jax.ShapeDtypeStruct((bsz, seq, 2 * d), BF16),
                 jax.ShapeDtypeStruct((bsz, seq, 2 * LANES), F32))
    out_specs = (tok(width), tok(3 * dnw), tok(dnw), tok(2 * d), tok(2 * LANES))
    return pl.pallas_call(
        functools.partial(_mix_in_kernel, n_sub=2),
        out_shape=out_shape,
        grid=(bsz, seq // tm),
        in_specs=in_specs,
        out_specs=out_specs,
        scratch_shapes=[pltpu.VMEM((CONV_HALO, 3 * dnw), F32)],
        compiler_params=pltpu.CompilerParams(
            dimension_semantics=("arbitrary", "arbitrary"),
            vmem_limit_bytes=VMEM_LIMIT_BYTES),
        name="mix_in",
    )(h, w_pack, conv_w, alog, dtb, lng, lnb, w_s, bsf)


def _dn_out_kernel(qkv_ref, gb_ref, oa_ref, sz_ref, gate_ref, x_ref, mod_ref, ng_ref,
                   wb0_ref, wb1_ref, wo_ref, o_ref, s_ref, ob_ref):
    bsz, tm, d = x_ref.shape
    dnw = sz_ref.shape[-1]
    hd = DN_HEAD_DIM
    n_chunks = tm // DN_CHUNK
    chains = [(b, h) for b in range(bsz) for h in range(DN_HEADS)]

    @pl.when(pl.program_id(0) == 0)
    def _():
        s_ref[...] = jnp.zeros(s_ref.shape, F32)

    row = lax.broadcasted_iota(jnp.int32, (tm, tm), 0)
    col = lax.broadcasted_iota(jnp.int32, (tm, tm), 1)
    shift = DN_CHUNK.bit_length() - 1
    same = jnp.right_shift(row, shift) == jnp.right_shift(col, shift)
    causal = jnp.logical_and(same, col <= row)
    cum_sel = jnp.where(causal, 1.0, 0.0).astype(BF16)
    e_row = lax.broadcasted_iota(jnp.int32, (2 * LANES, dnw), 0)
    e_col = lax.broadcasted_iota(jnp.int32, (2 * LANES, dnw), 1)
    expand = jnp.where(jnp.right_shift(e_col, hd.bit_length() - 1)
                       == jnp.bitwise_and(e_row, LANES - 1), 1.0, 0.0).astype(BF16)

    def lane_expand(x):
        hi = x.astype(BF16)
        mid = (x - hi.astype(F32)).astype(BF16)
        return _dot(jnp.concatenate([hi, mid], axis=1), expand)

    g_full, b_full, gc_t, eg_full = [], [], [], []
    for b in range(bsz):
        gc = _dot_sel_left(cum_sel, gb_ref[b, :, :LANES])
        g_full.append(lane_expand(gc))
        b_full.append(lane_expand(gb_ref[b, :, LANES:]))
        gc_t.append(gc.T)
        eg_full.append(jnp.exp(g_full[b]))

    k_b, kf, gh, egh, decay_t, rhs, qd_b = {}, {}, {}, {}, {}, {}, {}
    kk_t, qk_tb = {}, {}
    upper = jnp.logical_and(same, row <= col)
    strict_upper = jnp.logical_and(same, row < col)
    for c in chains:
        b, h = c
        cs = slice(h * hd, (h + 1) * hd)
        q = qkv_ref[b, :, h * hd:(h + 1) * hd]
        k_b[c] = qkv_ref[b, :, dnw + h * hd:dnw + (h + 1) * hd]
        v = qkv_ref[b, :, 2 * dnw + h * hd:2 * dnw + (h + 1) * hd]
        kf[c] = k_b[c].astype(F32)
        gh[c] = g_full[b][:, cs]
        egh[c] = eg_full[b][:, cs]
        bh = b_full[b][:, cs]
        kbeta = kf[c] * bh
        diff_t = gc_t[b][h:h + 1, :] - gh[c]
        decay_t[c] = jnp.where(upper, jnp.exp(jnp.where(upper, diff_t, 0.0)), 0.0)
        rhs[c] = jnp.concatenate([v.astype(F32) * bh, kbeta * egh[c]], axis=1)
        qd_b[c] = (q.astype(F32) * egh[c]).astype(BF16)
        both_t = _dot_nt(k_b[c], jnp.concatenate([kbeta.astype(BF16), q], axis=0))
        kk_t[c] = both_t[:, :tm]
        qk_tb[c] = (both_t[:, tm:] * decay_t[c]).astype(BF16)

    n, x = {}, {}
    for c in chains:
        a_up = jnp.where(strict_upper, kk_t[c] * decay_t[c], 0.0)
        a_b = a_up.astype(BF16)
        n[c] = -a_up
        x[c] = _dot(a_b, a_b)
    p = 2
    while p < DN_CHUNK:
        for c in chains:
            x_b = x[c].astype(BF16)
            n_b = n[c].astype(BF16)
            if 2 * p < DN_CHUNK:
                both = _dot(x_b, jnp.concatenate([x_b, n_b], axis=1))
                x_next, xn = both[:, :tm], both[:, tm:]
            else:
                x_next, xn = None, _dot(x_b, n_b)
            n[c] = n[c] + x[c] + xn
            x[c] = x_next
        p *= 2

    u, wq_b = {}, {}
    for c in chains:
        sol = rhs[c] + _dot_tn(n[c].astype(BF16), rhs[c].astype(BF16))
        u[c] = sol[:, :hd]
        w_b = sol[:, hd:].astype(BF16)
        wq_b[c] = [jnp.concatenate([w_b[i * DN_CHUNK:(i + 1) * DN_CHUNK],
                                    qd_b[c][i * DN_CHUNK:(i + 1) * DN_CHUNK]], axis=0)
                   for i in range(n_chunks)]

    state = {c: s_ref[c[0] * DN_HEADS + c[1]] for c in chains}
    o_inter = {c: [] for c in chains}
    v_new = {c: [] for c in chains}
    for i in range(n_chunks):
        rs = slice(i * DN_CHUNK, (i + 1) * DN_CHUNK)
        last = (i + 1) * DN_CHUNK - 1
        for c in chains:
            both = _dot(wq_b[c][i], state[c].astype(BF16))
            vn_b = (u[c][rs] - both[:DN_CHUNK]).astype(BF16)
            o_inter[c].append(both[DN_CHUNK:])
            kd_b = (kf[c][rs] * jnp.exp(gh[c][last:last + 1, :] - gh[c][rs])).astype(BF16)
            state[c] = state[c] * egh[c][last:last + 1, :] + _dot_tn(kd_b, vn_b)
            v_new[c].append(vn_b)
    for c in chains:
        b, h = c
        cs = slice(h * hd, (h + 1) * hd)
        s_ref[b * DN_HEADS + h] = state[c]
        o_h = (jnp.concatenate(o_inter[c], axis=0)
               + _dot_tn(qk_tb[c], jnp.concatenate(v_new[c], axis=0)))
        o_h = o_h * lax.rsqrt(jnp.mean(o_h * o_h, axis=-1, keepdims=True) + EPS) * ng_ref[...]
        ob_ref[b * tm:(b + 1) * tm, cs] = (o_h * sz_ref[b, :, cs].astype(F32)).astype(ob_ref.dtype)

    oa = jnp.concatenate([oa_ref[b] for b in range(bsz)], axis=0)
    gate = jnp.concatenate([gate_ref[b] for b in range(bsz)], axis=0).astype(F32)
    merged = gate[:, :d] * _dot(oa, wb0_ref[...]) + gate[:, d:] * _dot(ob_ref[...], wb1_ref[...])
    y = _dot(merged.astype(BF16), wo_ref[...])
    for b in range(bsz):
        ga = mod_ref[b, :, 5 * d:6 * d]
        o_ref[b] = x_ref[b] + ga * y[b * tm:(b + 1) * tm]


def _dn_out(qkv, gb, oa, sz, gate, x, mod3, ng, wb0, wb1, wo, *, tm):
    bsz, seq, d = x.shape
    dnw = sz.shape[-1]

    def tok(n):
        return pl.BlockSpec((bsz, tm, n), lambda i: (0, i, 0))

    in_specs = [tok(3 * dnw), tok(2 * LANES), tok(oa.shape[-1]), tok(dnw), tok(2 * d), tok(d),
                _const_spec(mod3.shape),
                _const_spec(ng.shape), _const_spec(wb0.shape), _const_spec(wb1.shape),
                _const_spec(wo.shape)]
    return pl.pallas_call(
        _dn_out_kernel,
        out_shape=jax.ShapeDtypeStruct((bsz, seq, d), F32),
        grid=(seq // tm,),
        in_specs=in_specs,
        out_specs=tok(d),
        scratch_shapes=[pltpu.VMEM((bsz * DN_HEADS, DN_HEAD_DIM, DN_HEAD_DIM), F32),
                        pltpu.VMEM((bsz * tm, dnw), BF16)],
        compiler_params=pltpu.CompilerParams(
            dimension_semantics=("arbitrary",),
            vmem_limit_bytes=VMEM_LIMIT_BYTES),
        name="dn_out",
    )(qkv, gb, oa, sz, gate, x, mod3, ng, wb0, wb1, wo)


def _pad_lanes(w):
    pad = LANES - w.shape[-1]
    return jnp.pad(w, [(0, 0)] * (w.ndim - 1) + [(0, pad)])


def kernel(x, c, w_ada, b_ada, norm1_g, ffn1_w_in, ffn1_w_out, norm2_g, w_in, conv_w, a_log,
           dt_bias, dn_norm_g, gm_ln_g, gm_ln_b, gm_w_s, gm_b_s, w_branch, w_out, norm3_g,
           ffn2_w_in, ffn2_w_out, final_g):
    bsz, seq, d = x.shape
    depth = w_ada.shape[0]
    assert depth == 1, "the final rmsnorm is fused into the last layer's second FFN"
    width = gm_ln_g.shape[-1]
    dnw = DN_HEADS * DN_HEAD_DIM
    tm_ffn = min(512, seq)
    tm_mix = min(512, seq)
    tm_dn = min(128, seq)

    for l in range(depth):
        mod3 = _ada(c, w_ada, b_ada.reshape(depth, 1, N_MOD * d), l).reshape(bsz, 1, N_MOD * d)
        x, h = _ffn(x, mod3, norm1_g[l], ffn1_w_in[l].astype(BF16), ffn1_w_out[l].astype(BF16),
                    norm2_g[l], k=0, post_k=1, tm=tm_ffn)

        wl = w_in[l]
        o_a = 2 * width + 4 * dnw
        w_pack = jnp.concatenate(
            [wl[:, :o_a], _pad_lanes(wl[:, o_a:o_a + DN_HEADS]),
             _pad_lanes(wl[:, o_a + DN_HEADS:o_a + 2 * DN_HEADS]), wl[:, o_a + 2 * DN_HEADS:]],
            axis=1).astype(BF16)
        bsf = jnp.repeat(gm_b_s[l].T, width // GM_GROUPS, axis=1)
        oa, qkv, sz, gate, gb = _mix_in(
            h, w_pack, conv_w[l],
            _pad_lanes(a_log[l].reshape(1, -1)), _pad_lanes(dt_bias[l].reshape(1, -1)),
            gm_ln_g[l].reshape(1, -1), gm_ln_b[l].reshape(1, -1), gm_w_s[l], bsf, tm=tm_mix)
        x = _dn_out(qkv, gb, oa, sz, gate, x, mod3, dn_norm_g[l].reshape(1, -1),
                    w_branch[l, 0].astype(BF16), w_branch[l, 1].astype(BF16),
                    w_out[l].astype(BF16), tm=tm_dn)

        x = _ffn(x, mod3, norm3_g[l], ffn2_w_in[l].astype(BF16), ffn2_w_out[l].astype(BF16),
                 final_g, k=2, post_k=None, tm=tm_ffn)
    return x
```

```python
import functools

import jax
import jax.numpy as jnp
from jax import lax
from jax.experimental import pallas as pl
from jax.experimental.pallas import tpu as pltpu

F32 = jnp.float32
BF16 = jnp.bfloat16

EPS = 1e-6
N_MOD = 9
LANES = 128
GM_CHUNK = 128
GM_GROUPS = 8
DN_HEADS = 8
DN_HEAD_DIM = 128
DN_CONV = 4
DN_CHUNK = 64
CONV_HALO = 8
VMEM_LIMIT_BYTES = 56 * 1024 * 1024


def _dot(a, b):
    return jnp.dot(a, b, preferred_element_type=F32)


def _dot_nt(a, b):
    return lax.dot_general(a, b, (((1,), (1,)), ((), ())), preferred_element_type=F32)


def _dot_tn(a, b):
    return lax.dot_general(a, b, (((0,), (0,)), ((), ())), preferred_element_type=F32)


def _split3(x):
    hi = x.astype(BF16)
    r1 = x - hi.astype(F32)
    mid = r1.astype(BF16)
    lo = (r1 - mid.astype(F32)).astype(BF16)
    return hi, mid, lo


def _dot_sel_left(sel, x):
    hi, mid, lo = _split3(x)
    return _dot(sel, hi) + _dot(sel, mid) + _dot(sel, lo)


def _sigmoid(x):
    return jax.nn.sigmoid(x)


def _silu(x):
    return x * _sigmoid(x)


def _gelu_exact(x):
    return 0.5 * x * (1.0 + lax.erf(x * (2.0 ** -0.5)))


def _rms(x, g):
    return x * lax.rsqrt(jnp.mean(x * x, axis=-1, keepdims=True) + EPS) * g


def _mod_slices(mod_ref, k, d):
    sh = mod_ref[:, (3 * k) * d:(3 * k + 1) * d]
    sc = mod_ref[:, (3 * k + 1) * d:(3 * k + 2) * d]
    ga = mod_ref[:, (3 * k + 2) * d:(3 * k + 3) * d]
    return sh, sc, ga


def _ada_kernel(c_ref, w_ref, b_ref, o_ref):
    ca = _silu(c_ref[...]).astype(BF16)
    o_ref[...] = _dot(ca, w_ref[...].astype(BF16)) + b_ref[...]


def _ada(c, w_ada, b_ada, layer):
    bsz, d = c.shape
    n = w_ada.shape[-1]
    tn = 1024
    c8 = jnp.zeros((8, d), F32).at[:bsz].set(c)
    out = pl.pallas_call(
        _ada_kernel,
        out_shape=jax.ShapeDtypeStruct((8, n), F32),
        grid=(n // tn,),
        in_specs=[pl.BlockSpec((8, d), lambda j: (0, 0)),
                  pl.BlockSpec((None, d, tn), lambda j: (layer, 0, j)),
                  pl.BlockSpec((None, 1, tn), lambda j: (layer, 0, j))],
        out_specs=pl.BlockSpec((8, tn), lambda j: (0, j)),
        compiler_params=pltpu.CompilerParams(
            dimension_semantics=("arbitrary",), vmem_limit_bytes=VMEM_LIMIT_BYTES),
        name="ada_mod",
    )(c8, w_ada, b_ada)
    return out[:bsz]


def _modulated_norm(x, g, mod_ref, k):
    sh, sc, _ = _mod_slices(mod_ref, k, x.shape[-1])
    return (_rms(x, g) * (1.0 + sc) + sh).astype(BF16)


def _ffn_kernel(x_ref, mod_ref, g_ref, win_ref, wout_ref, post_g_ref, *out_refs, k, post_k):
    d = x_ref.shape[-1]
    f = wout_ref.shape[0]
    x = x_ref[...]
    hb = _modulated_norm(x, g_ref[...], mod_ref, k)
    a = _dot(hb, win_ref[:, :f])
    b = _dot(hb, win_ref[:, f:])
    act = (_silu(a) * b).astype(BF16)
    ga = mod_ref[:, (3 * k + 2) * d:(3 * k + 3) * d]
    y = x + (0.5 * ga) * _dot(act, wout_ref[...])
    if post_k is None:
        out_refs[0][...] = _rms(y, post_g_ref[...])
    else:
        out_refs[0][...] = y
        out_refs[1][...] = _modulated_norm(y, post_g_ref[...], mod_ref, post_k)


def _const_spec(shape):
    nd = len(shape)
    return pl.BlockSpec(shape, lambda *_: (0,) * nd, pipeline_mode=pl.Buffered(1))


def _ffn(x, mod3, norm_g, w_in, w_out, post_g, *, k, post_k, tm):
    bsz, seq, d = x.shape
    f = w_out.shape[0]
    tok = pl.BlockSpec((None, tm, d), lambda b, i: (b, i, 0))
    in_specs = [tok,
                pl.BlockSpec((None, 1, N_MOD * d), lambda b, i: (b, 0, 0)),
                _const_spec((1, d)),
                _const_spec((d, 2 * f)),
                _const_spec((f, d)),
                _const_spec((1, d))]
    x_out = jax.ShapeDtypeStruct((bsz, seq, d), F32)
    if post_k is None:
        out_shape, out_specs = x_out, tok
    else:
        out_shape = (x_out, jax.ShapeDtypeStruct((bsz, seq, d), BF16))
        out_specs = (tok, tok)
    return pl.pallas_call(
        functools.partial(_ffn_kernel, k=k, post_k=post_k),
        out_shape=out_shape,
        grid=(bsz, seq // tm),
        in_specs=in_specs,
        out_specs=out_specs,
        compiler_params=pltpu.CompilerParams(
            dimension_semantics=("parallel", "parallel"),
            vmem_limit_bytes=VMEM_LIMIT_BYTES),
        name="ffn_final" if post_k is None else "ffn",
    )(x, mod3, norm_g.reshape(1, d), w_in, w_out, post_g.reshape(1, d))


def _causal_conv(pre, carry, convw_ref):
    w_taps = [convw_ref[DN_CONV - 1 - s:DN_CONV - s, :] for s in range(DN_CONV)]
    groups = [pre[r:r + CONV_HALO] for r in range(0, pre.shape[0], CONV_HALO)]
    out = [g * w_taps[0] for g in groups]
    sub = lax.broadcasted_iota(jnp.int32, carry.shape, 0)
    for s in range(1, DN_CONV):
        rot_prev = pltpu.roll(carry, s, 0)
        for r, g in enumerate(groups):
            rot = pltpu.roll(g, s, 0)
            out[r] = out[r] + jnp.where(sub < s, rot_prev, rot) * w_taps[s]
            rot_prev = rot
    return jnp.concatenate(out, axis=0)


def _mix_in_kernel(h_ref, w_ref, convw_ref, alog_ref, dtb_ref, lng_ref, lnb_ref, ws_ref,
                   bsf_ref, oa_ref, qkv_ref, sz_ref, gate_ref, gb_ref, cbuf_ref, *, n_sub):
    tm = h_ref.shape[0]
    width = oa_ref.shape[-1]
    dnw = sz_ref.shape[-1]
    ts = tm // n_sub
    gd = width // GM_GROUPS
    o_qkv = 2 * width
    o_z = o_qkv + 3 * dnw
    o_ab = o_z + dnw
    o_gate = o_ab + 2 * LANES

    @pl.when(pl.program_id(1) == 0)
    def _():
        cbuf_ref[...] = jnp.zeros(cbuf_ref.shape, F32)

    row = lax.broadcasted_iota(jnp.int32, (GM_CHUNK, GM_CHUNK), 0)
    col = lax.broadcasted_iota(jnp.int32, (GM_CHUNK, GM_CHUNK), 1)
    tril = row >= col
    w_mix = [jnp.where(tril, ws_ref[g], 0.0).astype(BF16) for g in range(GM_GROUPS)]
    carry = cbuf_ref[...]

    for r0 in range(0, tm, ts):
        rows = slice(r0, r0 + ts)
        hb = h_ref[rows, :]

        uv = _dot(hb, w_ref[:, :o_qkv])
        pre = _dot(hb, w_ref[:, o_qkv:o_z])
        z = _dot(hb, w_ref[:, o_z:o_ab])
        ab = _dot(hb, w_ref[:, o_ab:o_gate])
        gate_logits = _dot(hb, w_ref[:, o_gate:])

        u = _gelu_exact(uv[:, :width])
        v = _gelu_exact(uv[:, width:])
        mu = jnp.mean(v, axis=-1, keepdims=True)
        vc = v - mu
        var = jnp.mean(vc * vc, axis=-1, keepdims=True)
        vb = (vc * lax.rsqrt(var + EPS) * lng_ref[...] + lnb_ref[...]).astype(BF16)
        for g in range(GM_GROUPS):
            cs = slice(g * gd, (g + 1) * gd)
            for c0 in range(0, ts, 2 * GM_CHUNK):
                pair = jnp.concatenate([vb[c0:c0 + GM_CHUNK, cs],
                                        vb[c0 + GM_CHUNK:c0 + 2 * GM_CHUNK, cs]], axis=1)
                mixed = _dot(w_mix[g], pair)
                for j in range(2):
                    cr = slice(c0 + j * GM_CHUNK, c0 + (j + 1) * GM_CHUNK)
                    mix_j = mixed[:, j * gd:(j + 1) * gd] + bsf_ref[:, cs]
                    oa_ref[r0 + cr.start:r0 + cr.stop, cs] = (
                        u[cr, cs] * mix_j).astype(oa_ref.dtype)

        act = _silu(_causal_conv(pre, carry, convw_ref))
        carry = pre[ts - CONV_HALO:ts, :]
        for h in range(DN_HEADS):
            for part in range(2):
                cs = slice(part * dnw + h * DN_HEAD_DIM, part * dnw + (h + 1) * DN_HEAD_DIM)
                t = act[:, cs]
                t = t * lax.rsqrt(jnp.sum(t * t, axis=-1, keepdims=True) + EPS)
                if part == 0:
                    t = t * (DN_HEAD_DIM ** -0.5)
                qkv_ref[rows, cs] = t.astype(qkv_ref.dtype)
        qkv_ref[rows, 2 * dnw:] = act[:, 2 * dnw:].astype(qkv_ref.dtype)

        sz_ref[rows, :] = _silu(z).astype(sz_ref.dtype)
        gate_ref[rows, :] = _sigmoid(gate_logits).astype(gate_ref.dtype)
        gb_ref[rows, :LANES] = (-jnp.exp(alog_ref[...])
                                * jax.nn.softplus(ab[:, :LANES] + dtb_ref[...]))
        gb_ref[rows, LANES:] = _sigmoid(ab[:, LANES:])
    cbuf_ref[...] = carry


def _mix_in(h, w_pack, conv_w, alog, dtb, lng, lnb, w_s, bsf, *, tm):
    bsz, seq, d = h.shape
    width = lng.shape[-1]
    dnw = conv_w.shape[-1] // 3

    def tok(n):
        return pl.BlockSpec((None, tm, n), lambda b, i: (b, i, 0))

    in_specs = [tok(d),
                _const_spec(w_pack.shape),
                _const_spec(conv_w.shape), _const_spec(alog.shape), _const_spec(dtb.shape),
                _const_spec(lng.shape), _const_spec(lnb.shape),
                _const_spec(w_s.shape), _const_spec(bsf.shape)]
    out_shape = (jax.ShapeDtypeStruct((bsz, seq, width), BF16),
                 jax.ShapeDtypeStruct((bsz, seq, 3 * dnw), BF16),
                 jax.ShapeDtypeStruct((bsz, seq, dnw), BF16),
                 jax.ShapeDtypeStruct((bsz, seq, 2 * d), BF16),
                 jax.ShapeDtypeStruct((bsz, seq, 2 * LANES), F32))
    out_specs = (tok(width), tok(3 * dnw), tok(dnw), tok(2 * d), tok(2 * LANES))
    return pl.pallas_call(
        functools.partial(_mix_in_kernel, n_sub=1),
        out_shape=out_shape,
        grid=(bsz, seq // tm),
        in_specs=in_specs,
        out_specs=out_specs,
        scratch_shapes=[pltpu.VMEM((CONV_HALO, 3 * dnw), F32)],
        compiler_params=pltpu.CompilerParams(
            dimension_semantics=("arbitrary", "arbitrary"),
            vmem_limit_bytes=VMEM_LIMIT_BYTES),
        name="mix_in",
    )(h, w_pack, conv_w, alog, dtb, lng, lnb, w_s, bsf)


def _pack_w_in_kernel(w_ref, o_ref, *, o_a):
    n_h = DN_HEADS
    o_ref[:, :o_a] = w_ref[:, :o_a].astype(o_ref.dtype)
    lane = lax.broadcasted_iota(jnp.int32, (w_ref.shape[0], LANES), 1)
    for j in range(2):
        cols = w_ref[:, o_a + j * n_h:o_a + j * n_h + LANES]
        o_ref[:, o_a + j * LANES:o_a + (j + 1) * LANES] = jnp.where(
            lane < n_h, cols, 0.0).astype(o_ref.dtype)
    o_ref[:, o_a + 2 * LANES:] = w_ref[:, o_a + 2 * n_h:].astype(o_ref.dtype)


def _pack_w_in(w_in, layer, o_a):
    _, d, n = w_in.shape
    n_out = n - 2 * DN_HEADS + 2 * LANES
    tr = 256
    return pl.pallas_call(
        functools.partial(_pack_w_in_kernel, o_a=o_a),
        out_shape=jax.ShapeDtypeStruct((d, n_out), BF16),
        grid=(d // tr,),
        in_specs=[pl.BlockSpec((None, tr, n), lambda i: (layer, i, 0))],
        out_specs=pl.BlockSpec((tr, n_out), lambda i: (i, 0)),
        compiler_params=pltpu.CompilerParams(
            dimension_semantics=("arbitrary",), vmem_limit_bytes=VMEM_LIMIT_BYTES),
        name="pack_w_in",
    )(w_in)


def _dn_out_kernel(qkv_ref, gb_ref, oa_ref, sz_ref, gate_ref, x_ref, mod_ref, ng_ref,
                   wb0_ref, wb1_ref, wo_ref, o_ref, s_ref, ob_ref):
    bsz, tm, d = x_ref.shape
    dnw = sz_ref.shape[-1]
    hd = DN_HEAD_DIM
    n_chunks = tm // DN_CHUNK
    chains = [(b, h) for b in range(bsz) for h in range(DN_HEADS)]

    @pl.when(pl.program_id(0) == 0)
    def _():
        s_ref[...] = jnp.zeros(s_ref.shape, F32)

    row = lax.broadcasted_iota(jnp.int32, (tm, tm), 0)
    col = lax.broadcasted_iota(jnp.int32, (tm, tm), 1)
    shift = DN_CHUNK.bit_length() - 1
    same = jnp.right_shift(row, shift) == jnp.right_shift(col, shift)
    causal = jnp.logical_and(same, col <= row)
    cum_sel = jnp.where(causal, 1.0, 0.0).astype(BF16)
    e_row = lax.broadcasted_iota(jnp.int32, (2 * LANES, dnw), 0)
    e_col = lax.broadcasted_iota(jnp.int32, (2 * LANES, dnw), 1)
    expand = jnp.where(jnp.right_shift(e_col, hd.bit_length() - 1)
                       == jnp.bitwise_and(e_row, LANES - 1), 1.0, 0.0).astype(BF16)

    def lane_expand(x):
        hi = x.astype(BF16)
        mid = (x - hi.astype(F32)).astype(BF16)
        return _dot(jnp.concatenate([hi, mid], axis=1), expand)

    g_full, b_full, gc_t, eg_full = [], [], [], []
    for b in range(bsz):
        gc = _dot_sel_left(cum_sel, gb_ref[b, :, :LANES])
        g_full.append(lane_expand(gc))
        b_full.append(lane_expand(gb_ref[b, :, LANES:]))
        gc_t.append(gc.T)
        eg_full.append(jnp.exp(g_full[b]))

    k_b, kf, gh, egh, decay_t, rhs, qd_b = {}, {}, {}, {}, {}, {}, {}
    kk_t, qk_tb = {}, {}
    upper = jnp.logical_and(same, row <= col)
    strict_upper = jnp.logical_and(same, row < col)
    for c in chains:
        b, h = c
        cs = slice(h * hd, (h + 1) * hd)
        q = qkv_ref[b, :, h * hd:(h + 1) * hd]
        k_b[c] = qkv_ref[b, :, dnw + h * hd:dnw + (h + 1) * hd]
        v = qkv_ref[b, :, 2 * dnw + h * hd:2 * dnw + (h + 1) * hd]
        kf[c] = k_b[c].astype(F32)
        gh[c] = g_full[b][:, cs]
        egh[c] = eg_full[b][:, cs]
        bh = b_full[b][:, cs]
        kbeta = kf[c] * bh
        diff_t = gc_t[b][h:h + 1, :] - gh[c]
        decay_t[c] = jnp.where(upper, jnp.exp(jnp.where(upper, diff_t, 0.0)), 0.0)
        rhs[c] = jnp.concatenate([v.astype(F32) * bh, kbeta * egh[c]], axis=1)
        qd_b[c] = (q.astype(F32) * egh[c]).astype(BF16)
        both_t = _dot_nt(k_b[c], jnp.concatenate([kbeta.astype(BF16), q], axis=0))
        kk_t[c] = both_t[:, :tm]
        qk_tb[c] = (both_t[:, tm:] * decay_t[c]).astype(BF16)

    n, x = {}, {}
    for c in chains:
        a_up = jnp.where(strict_upper, kk_t[c] * decay_t[c], 0.0)
        a_b = a_up.astype(BF16)
        n[c] = -a_up
        x[c] = _dot(a_b, a_b)
    p = 2
    while p < DN_CHUNK:
        for c in chains:
            x_b = x[c].astype(BF16)
            n_b = n[c].astype(BF16)
            if 2 * p < DN_CHUNK:
                both = _dot(x_b, jnp.concatenate([x_b, n_b], axis=1))
                x_next, xn = both[:, :tm], both[:, tm:]
            else:
                x_next, xn = None, _dot(x_b, n_b)
            n[c] = n[c] + x[c] + xn
            x[c] = x_next
        p *= 2

    u, wq_b = {}, {}
    for c in chains:
        sol = rhs[c] + _dot_tn(n[c].astype(BF16), rhs[c].astype(BF16))
        u[c] = sol[:, :hd]
        w_b = sol[:, hd:].astype(BF16)
        wq_b[c] = [jnp.concatenate([w_b[i * DN_CHUNK:(i + 1) * DN_CHUNK],
                                    qd_b[c][i * DN_CHUNK:(i + 1) * DN_CHUNK]], axis=0)
                   for i in range(n_chunks)]

    state = {c: s_ref[c[0] * DN_HEADS + c[1]] for c in chains}
    o_inter = {c: [] for c in chains}
    v_new = {c: [] for c in chains}
    for i in range(n_chunks):
        rs = slice(i * DN_CHUNK, (i + 1) * DN_CHUNK)
        last = (i + 1) * DN_CHUNK - 1
        for c in chains:
            both = _dot(wq_b[c][i], state[c].astype(BF16))
            vn_b = (u[c][rs] - both[:DN_CHUNK]).astype(BF16)
            o_inter[c].append(both[DN_CHUNK:])
            kd_b = (kf[c][rs] * jnp.exp(gh[c][last:last + 1, :] - gh[c][rs])).astype(BF16)
            state[c] = state[c] * egh[c][last:last + 1, :] + _dot_tn(kd_b, vn_b)
            v_new[c].append(vn_b)
    for c in chains:
        b, h = c
        cs = slice(h * hd, (h + 1) * hd)
        s_ref[b * DN_HEADS + h] = state[c]
        o_h = (jnp.concatenate(o_inter[c], axis=0)
               + _dot_tn(qk_tb[c], jnp.concatenate(v_new[c], axis=0)))
        o_h = o_h * lax.rsqrt(jnp.mean(o_h * o_h, axis=-1, keepdims=True) + EPS) * ng_ref[...]
        ob_ref[b * tm:(b + 1) * tm, cs] = (o_h * sz_ref[b, :, cs].astype(F32)).astype(ob_ref.dtype)

    oa = jnp.concatenate([oa_ref[b] for b in range(bsz)], axis=0)
    gate = jnp.concatenate([gate_ref[b] for b in range(bsz)], axis=0).astype(F32)
    merged = gate[:, :d] * _dot(oa, wb0_ref[...]) + gate[:, d:] * _dot(ob_ref[...], wb1_ref[...])
    y = _dot(merged.astype(BF16), wo_ref[...])
    for b in range(bsz):
        ga = mod_ref[b, :, 5 * d:6 * d]
        o_ref[b] = x_ref[b] + ga * y[b * tm:(b + 1) * tm]


def _dn_out(qkv, gb, oa, sz, gate, x, mod3, ng, wb0, wb1, wo, *, tm):
    bsz, seq, d = x.shape
    dnw = sz.shape[-1]

    def tok(n):
        return pl.BlockSpec((bsz, tm, n), lambda i: (0, i, 0))

    in_specs = [tok(3 * dnw), tok(2 * LANES), tok(oa.shape[-1]), tok(dnw), tok(2 * d), tok(d),
                _const_spec(mod3.shape),
                _const_spec(ng.shape), _const_spec(wb0.shape), _const_spec(wb1.shape),
                _const_spec(wo.shape)]
    return pl.pallas_call(
        _dn_out_kernel,
        out_shape=jax.ShapeDtypeStruct((bsz, seq, d), F32),
        grid=(seq // tm,),
        in_specs=in_specs,
        out_specs=tok(d),
        scratch_shapes=[pltpu.VMEM((bsz * DN_HEADS, DN_HEAD_DIM, DN_HEAD_DIM), F32),
                        pltpu.VMEM((bsz * tm, dnw), BF16)],
        compiler_params=pltpu.CompilerParams(
            dimension_semantics=("arbitrary",),
            vmem_limit_bytes=VMEM_LIMIT_BYTES),
        name="dn_out",
    )(qkv, gb, oa, sz, gate, x, mod3, ng, wb0, wb1, wo)


def _pad_lanes(w):
    pad = LANES - w.shape[-1]
    return jnp.pad(w, [(0, 0)] * (w.ndim - 1) + [(0, pad)])


def kernel(x, c, w_ada, b_ada, norm1_g, ffn1_w_in, ffn1_w_out, norm2_g, w_in, conv_w, a_log,
           dt_bias, dn_norm_g, gm_ln_g, gm_ln_b, gm_w_s, gm_b_s, w_branch, w_out, norm3_g,
           ffn2_w_in, ffn2_w_out, final_g):
    bsz, seq, d = x.shape
    depth = w_ada.shape[0]
    assert depth == 1, "the final rmsnorm is fused into the last layer's second FFN"
    width = gm_ln_g.shape[-1]
    dnw = DN_HEADS * DN_HEAD_DIM
    tm_ffn = min(512, seq)
    tm_mix = min(512, seq)
    tm_dn = min(128, seq)

    for l in range(depth):
        mod3 = _ada(c, w_ada, b_ada.reshape(depth, 1, N_MOD * d), l).reshape(bsz, 1, N_MOD * d)
        x, h = _ffn(x, mod3, norm1_g[l], ffn1_w_in[l].astype(BF16), ffn1_w_out[l].astype(BF16),
                    norm2_g[l], k=0, post_k=1, tm=tm_ffn)

        w_pack = _pack_w_in(w_in, l, 2 * width + 4 * dnw)
        bsf = jnp.repeat(gm_b_s[l].T, width // GM_GROUPS, axis=1)
        oa, qkv, sz, gate, gb = _mix_in(
            h, w_pack, conv_w[l],
            _pad_lanes(a_log[l].reshape(1, -1)), _pad_lanes(dt_bias[l].reshape(1, -1)),
            gm_ln_g[l].reshape(1, -1), gm_ln_b[l].reshape(1, -1), gm_w_s[l], bsf, tm=tm_mix)
        x = _dn_out(qkv, gb, oa, sz, gate, x, mod3, dn_norm_g[l].reshape(1, -1),
                    w_branch[l, 0].astype(BF16), w_branch[l, 1].astype(BF16),
                    w_out[l].astype(BF16), tm=tm_dn)

        x = _ffn(x, mod3, norm3_g[l], ffn2_w_in[l].astype(BF16), ffn2_w_out[l].astype(BF16),
                 final_g, k=2, post_k=None, tm=tm_ffn)
    return x
```

```python
import functools

import jax
import jax.numpy as jnp
from jax import lax
from jax.experimental import pallas as pl
from jax.experimental.pallas import tpu as pltpu

F32 = jnp.float32
BF16 = jnp.bfloat16

EPS = 1e-6
N_MOD = 9
LANES = 128
GM_CHUNK = 128
GM_GROUPS = 8
DN_HEADS = 8
DN_HEAD_DIM = 128
DN_CONV = 4
DN_CHUNK = 64
DN_BLOCK = 128
CONV_HALO = 8
VMEM_LIMIT_BYTES = 56 * 1024 * 1024


def _dot(a, b):
    return jnp.dot(a, b, preferred_element_type=F32)


def _dot_nt(a, b):
    return lax.dot_general(a, b, (((1,), (1,)), ((), ())), preferred_element_type=F32)


def _dot_tn(a, b):
    return lax.dot_general(a, b, (((0,), (0,)), ((), ())), preferred_element_type=F32)


def _split3(x):
    hi = x.astype(BF16)
    r1 = x - hi.astype(F32)
    mid = r1.astype(BF16)
    lo = (r1 - mid.astype(F32)).astype(BF16)
    return hi, mid, lo


def _dot_sel_left(sel, x):
    hi, mid, lo = _split3(x)
    return _dot(sel, hi) + _dot(sel, mid) + _dot(sel, lo)


def _sigmoid(x):
    return jax.nn.sigmoid(x)


def _silu(x):
    return x * _sigmoid(x)


def _gelu_exact(x):
    return 0.5 * x * (1.0 + lax.erf(x * (2.0 ** -0.5)))


def _rms(x, g):
    return x * lax.rsqrt(jnp.mean(x * x, axis=-1, keepdims=True) + EPS) * g


def _mod_slices(mod_ref, k, d):
    sh = mod_ref[:, (3 * k) * d:(3 * k + 1) * d]
    sc = mod_ref[:, (3 * k + 1) * d:(3 * k + 2) * d]
    ga = mod_ref[:, (3 * k + 2) * d:(3 * k + 3) * d]
    return sh, sc, ga


def _ada_kernel(c_ref, w_ref, b_ref, o_ref):
    ca = _silu(c_ref[...]).astype(BF16)
    o_ref[...] = _dot(ca, w_ref[...].astype(BF16)) + b_ref[...]


def _ada(c, w_ada, b_ada, layer):
    bsz, d = c.shape
    n = w_ada.shape[-1]
    tn = 1024
    c8 = jnp.zeros((8, d), F32).at[:bsz].set(c)
    out = pl.pallas_call(
        _ada_kernel,
        out_shape=jax.ShapeDtypeStruct((8, n), F32),
        grid=(n // tn,),
        in_specs=[pl.BlockSpec((8, d), lambda j: (0, 0)),
                  pl.BlockSpec((None, d, tn), lambda j: (layer, 0, j)),
                  pl.BlockSpec((None, 1, tn), lambda j: (layer, 0, j))],
        out_specs=pl.BlockSpec((8, tn), lambda j: (0, j)),
        compiler_params=pltpu.CompilerParams(
            dimension_semantics=("arbitrary",), vmem_limit_bytes=VMEM_LIMIT_BYTES),
        name="ada_mod",
    )(c8, w_ada, b_ada)
    return out[:bsz]


def _modulated_norm(x, g, mod_ref, k):
    sh, sc, _ = _mod_slices(mod_ref, k, x.shape[-1])
    return (_rms(x, g) * (1.0 + sc) + sh).astype(BF16)


def _ffn_kernel(x_ref, mod_ref, g_ref, win_ref, wout_ref, post_g_ref, *out_refs, k, post_k):
    d = x_ref.shape[-1]
    f = wout_ref.shape[0]
    x = x_ref[...]
    hb = _modulated_norm(x, g_ref[...], mod_ref, k)
    a = _dot(hb, win_ref[:, :f])
    b = _dot(hb, win_ref[:, f:])
    act = (_silu(a) * b).astype(BF16)
    ga = mod_ref[:, (3 * k + 2) * d:(3 * k + 3) * d]
    y = x + (0.5 * ga) * _dot(act, wout_ref[...])
    if post_k is None:
        out_refs[0][...] = _rms(y, post_g_ref[...])
    else:
        out_refs[0][...] = y
        out_refs[1][...] = _modulated_norm(y, post_g_ref[...], mod_ref, post_k)


def _const_spec(shape):
    nd = len(shape)
    return pl.BlockSpec(shape, lambda *_: (0,) * nd, pipeline_mode=pl.Buffered(1))


def _ffn(x, mod3, norm_g, w_in, w_out, post_g, *, k, post_k, tm):
    bsz, seq, d = x.shape
    f = w_out.shape[0]
    tok = pl.BlockSpec((None, tm, d), lambda b, i: (b, i, 0))
    in_specs = [tok,
                pl.BlockSpec((None, 1, N_MOD * d), lambda b, i: (b, 0, 0)),
                _const_spec((1, d)),
                _const_spec((d, 2 * f)),
                _const_spec((f, d)),
                _const_spec((1, d))]
    x_out = jax.ShapeDtypeStruct((bsz, seq, d), F32)
    if post_k is None:
        out_shape, out_specs = x_out, tok
    else:
        out_shape = (x_out, jax.ShapeDtypeStruct((bsz, seq, d), BF16))
        out_specs = (tok, tok)
    return pl.pallas_call(
        functools.partial(_ffn_kernel, k=k, post_k=post_k),
        out_shape=out_shape,
        grid=(bsz, seq // tm),
        in_specs=in_specs,
        out_specs=out_specs,
        compiler_params=pltpu.CompilerParams(
            dimension_semantics=("parallel", "parallel"),
            vmem_limit_bytes=VMEM_LIMIT_BYTES),
        name="ffn_final" if post_k is None else "ffn",
    )(x, mod3, norm_g.reshape(1, d), w_in, w_out, post_g.reshape(1, d))


def _causal_conv(pre, carry, convw_ref):
    w_taps = [convw_ref[DN_CONV - 1 - s:DN_CONV - s, :] for s in range(DN_CONV)]
    groups = [pre[r:r + CONV_HALO] for r in range(0, pre.shape[0], CONV_HALO)]
    out = [g * w_taps[0] for g in groups]
    sub = lax.broadcasted_iota(jnp.int32, carry.shape, 0)
    for s in range(1, DN_CONV):
        rot_prev = pltpu.roll(carry, s, 0)
        for r, g in enumerate(groups):
            rot = pltpu.roll(g, s, 0)
            out[r] = out[r] + jnp.where(sub < s, rot_prev, rot) * w_taps[s]
            rot_prev = rot
    return jnp.concatenate(out, axis=0)


def _mix_in_kernel(h_ref, w_ref, wt_ref, convw_ref, alog_ref, dtb_ref, lng_ref, lnb_ref,
                   ws_ref, bsf_ref, oa_ref, qkv_ref, sz_ref, gate_ref, gb_ref, cbuf_ref, *,
                   n_sub):
    tm = h_ref.shape[0]
    width = oa_ref.shape[-1]
    dnw = sz_ref.shape[-1]
    ts = tm // n_sub
    gd = width // GM_GROUPS
    o_qkv = 2 * width
    o_z = o_qkv + 3 * dnw
    o_gate = 2 * LANES

    @pl.when(pl.program_id(1) == 0)
    def _():
        cbuf_ref[...] = jnp.zeros(cbuf_ref.shape, F32)

    row = lax.broadcasted_iota(jnp.int32, (GM_CHUNK, GM_CHUNK), 0)
    col = lax.broadcasted_iota(jnp.int32, (GM_CHUNK, GM_CHUNK), 1)
    tril = row >= col
    w_mix = [jnp.where(tril, ws_ref[g], 0.0).astype(BF16) for g in range(GM_GROUPS)]
    carry = cbuf_ref[...]

    for r0 in range(0, tm, ts):
        rows = slice(r0, r0 + ts)
        hb = h_ref[rows, :]

        uv = _dot(hb, w_ref[:, :o_qkv])
        pre = _dot(hb, w_ref[:, o_qkv:o_z])
        z = _dot(hb, w_ref[:, o_z:])
        ab = _dot(hb, wt_ref[:, :o_gate])
        gate_logits = _dot(hb, wt_ref[:, o_gate:])

        u = _gelu_exact(uv[:, :width])
        v = _gelu_exact(uv[:, width:])
        mu = jnp.mean(v, axis=-1, keepdims=True)
        vc = v - mu
        var = jnp.mean(vc * vc, axis=-1, keepdims=True)
        vb = (vc * lax.rsqrt(var + EPS) * lng_ref[...] + lnb_ref[...]).astype(BF16)
        for g in range(GM_GROUPS):
            cs = slice(g * gd, (g + 1) * gd)
            for c0 in range(0, ts, 2 * GM_CHUNK):
                pair = jnp.concatenate([vb[c0:c0 + GM_CHUNK, cs],
                                        vb[c0 + GM_CHUNK:c0 + 2 * GM_CHUNK, cs]], axis=1)
                mixed = _dot(w_mix[g], pair)
                for j in range(2):
                    cr = slice(c0 + j * GM_CHUNK, c0 + (j + 1) * GM_CHUNK)
                    mix_j = mixed[:, j * gd:(j + 1) * gd] + bsf_ref[:, cs]
                    oa_ref[r0 + cr.start:r0 + cr.stop, cs] = (
                        u[cr, cs] * mix_j).astype(oa_ref.dtype)

        act = _silu(_causal_conv(pre, carry, convw_ref))
        carry = pre[ts - CONV_HALO:ts, :]
        for h in range(DN_HEADS):
            for part in range(2):
                cs = slice(part * dnw + h * DN_HEAD_DIM, part * dnw + (h + 1) * DN_HEAD_DIM)
                t = act[:, cs]
                t = t * lax.rsqrt(jnp.sum(t * t, axis=-1, keepdims=True) + EPS)
                if part == 0:
                    t = t * (DN_HEAD_DIM ** -0.5)
                qkv_ref[rows, cs] = t.astype(qkv_ref.dtype)
        qkv_ref[rows, 2 * dnw:] = act[:, 2 * dnw:].astype(qkv_ref.dtype)

        sz_ref[rows, :] = _silu(z).astype(sz_ref.dtype)
        gate_ref[rows, :] = _sigmoid(gate_logits).astype(gate_ref.dtype)
        gb_ref[rows, :LANES] = (-jnp.exp(alog_ref[...])
                                * jax.nn.softplus(ab[:, :LANES] + dtb_ref[...]))
        gb_ref[rows, LANES:] = _sigmoid(ab[:, LANES:])
    cbuf_ref[...] = carry


def _mix_in(h, w_main, w_tail, conv_w, alog, dtb, lng, lnb, w_s, bsf, *, tm):
    bsz, seq, d = h.shape
    width = lng.shape[-1]
    dnw = conv_w.shape[-1] // 3

    def tok(n):
        return pl.BlockSpec((None, tm, n), lambda b, i: (b, i, 0))

    in_specs = [tok(d),
                _const_spec(w_main.shape), _const_spec(w_tail.shape),
                _const_spec(conv_w.shape), _const_spec(alog.shape), _const_spec(dtb.shape),
                _const_spec(lng.shape), _const_spec(lnb.shape),
                _const_spec(w_s.shape), _const_spec(bsf.shape)]
    out_shape = (jax.ShapeDtypeStruct((bsz, seq, width), BF16),
                 jax.ShapeDtypeStruct((bsz, seq, 3 * dnw), BF16),
                 jax.ShapeDtypeStruct((bsz, seq, dnw), BF16),
                 jax.ShapeDtypeStruct((bsz, seq, 2 * d), BF16),
                 jax.ShapeDtypeStruct((bsz, seq, 2 * LANES), F32))
    out_specs = (tok(width), tok(3 * dnw), tok(dnw), tok(2 * d), tok(2 * LANES))
    return pl.pallas_call(
        functools.partial(_mix_in_kernel, n_sub=2),
        out_shape=out_shape,
        grid=(bsz, seq // tm),
        in_specs=in_specs,
        out_specs=out_specs,
        scratch_shapes=[pltpu.VMEM((CONV_HALO, 3 * dnw), F32)],
        compiler_params=pltpu.CompilerParams(
            dimension_semantics=("arbitrary", "arbitrary"),
            vmem_limit_bytes=VMEM_LIMIT_BYTES),
        name="mix_in",
    )(h, w_main, w_tail, conv_w, alog, dtb, lng, lnb, w_s, bsf)


def _dn_out_kernel(qkv_ref, gb_ref, oa_ref, sz_ref, gate_ref, x_ref, mod_ref, ng_ref,
                   wb0_ref, wb1_ref, wo_ref, o_ref, s_ref, ob_ref):
    bsz, tm, d = x_ref.shape
    dnw = sz_ref.shape[-1]
    hd = DN_HEAD_DIM
    blk = DN_BLOCK
    blocks = range(tm // blk)
    chunks = range(blk // DN_CHUNK)
    chains = [(b, h) for b in range(bsz) for h in range(DN_HEADS)]
    items = [(b, h, j) for (b, h) in chains for j in blocks]

    @pl.when(pl.program_id(0) == 0)
    def _():
        s_ref[...] = jnp.zeros(s_ref.shape, F32)

    row = lax.broadcasted_iota(jnp.int32, (blk, blk), 0)
    col = lax.broadcasted_iota(jnp.int32, (blk, blk), 1)
    shift = DN_CHUNK.bit_length() - 1
    same = jnp.right_shift(row, shift) == jnp.right_shift(col, shift)
    cum_sel = jnp.where(jnp.logical_and(same, col <= row), 1.0, 0.0).astype(BF16)
    upper = jnp.logical_and(same, row <= col)
    strict_upper = jnp.logical_and(same, row < col)
    e_row = lax.broadcasted_iota(jnp.int32, (2 * LANES, dnw), 0)
    e_col = lax.broadcasted_iota(jnp.int32, (2 * LANES, dnw), 1)
    expand = jnp.where(jnp.right_shift(e_col, hd.bit_length() - 1)
                       == jnp.bitwise_and(e_row, LANES - 1), 1.0, 0.0).astype(BF16)

    def lane_expand(x):
        hi = x.astype(BF16)
        mid = (x - hi.astype(F32)).astype(BF16)
        return _dot(jnp.concatenate([hi, mid], axis=1), expand)

    g_full, b_full, gc_t, eg_full = [], [], [], []
    for b in range(bsz):
        gc = [_dot_sel_left(cum_sel, gb_ref[b, j * blk:(j + 1) * blk, :LANES]) for j in blocks]
        g_full.append(lane_expand(jnp.concatenate(gc, axis=0)))
        b_full.append(lane_expand(gb_ref[b, :, LANES:]))
        gc_t.append([g.T for g in gc])
        eg_full.append(jnp.exp(g_full[b]))

    k_b, kf, gh, egh, decay_t, rhs, qd_b = {}, {}, {}, {}, {}, {}, {}
    kk_t, qk_tb = {}, {}
    for c in items:
        b, h, j = c
        rows = slice(j * blk, (j + 1) * blk)
        cs = slice(h * hd, (h + 1) * hd)
        q = qkv_ref[b, rows, h * hd:(h + 1) * hd]
        k_b[c] = qkv_ref[b, rows, dnw + h * hd:dnw + (h + 1) * hd]
        v = qkv_ref[b, rows, 2 * dnw + h * hd:2 * dnw + (h + 1) * hd]
        kf[c] = k_b[c].astype(F32)
        gh[c] = g_full[b][rows, cs]
        egh[c] = eg_full[b][rows, cs]
        bh = b_full[b][rows, cs]
        kbeta = kf[c] * bh
        diff_t = gc_t[b][j][h:h + 1, :] - gh[c]
        decay_t[c] = jnp.where(upper, jnp.exp(jnp.where(upper, diff_t, 0.0)), 0.0)
        rhs[c] = jnp.concatenate([v.astype(F32) * bh, kbeta * egh[c]], axis=1)
        qd_b[c] = (q.astype(F32) * egh[c]).astype(BF16)
        both_t = _dot_nt(k_b[c], jnp.concatenate([kbeta.astype(BF16), q], axis=0))
        kk_t[c] = both_t[:, :blk]
        qk_tb[c] = (both_t[:, blk:] * decay_t[c]).astype(BF16)

    n, x = {}, {}
    for c in items:
        a_up = jnp.where(strict_upper, kk_t[c] * decay_t[c], 0.0)
        a_b = a_up.astype(BF16)
        n[c] = -a_up
        x[c] = _dot(a_b, a_b)
    p = 2
    while p < DN_CHUNK:
        for c in items:
            x_b = x[c].astype(BF16)
            n_b = n[c].astype(BF16)
            if 2 * p < DN_CHUNK:
                both = _dot(x_b, jnp.concatenate([x_b, n_b], axis=1))
                x_next, xn = both[:, :blk], both[:, blk:]
            else:
                x_next, xn = None, _dot(x_b, n_b)
            n[c] = n[c] + x[c] + xn
            x[c] = x_next
        p *= 2

    u, wq_b = {}, {}
    for c in items:
        sol = rhs[c] + _dot_tn(n[c].astype(BF16), rhs[c].astype(BF16))
        u[c] = sol[:, :hd]
        w_b = sol[:, hd:].astype(BF16)
        wq_b[c] = [jnp.concatenate([w_b[i * DN_CHUNK:(i + 1) * DN_CHUNK],
                                    qd_b[c][i * DN_CHUNK:(i + 1) * DN_CHUNK]], axis=0)
                   for i in chunks]

    state = {bh_: s_ref[bh_[0] * DN_HEADS + bh_[1]] for bh_ in chains}
    o_inter = {c: [] for c in items}
    v_new = {c: [] for c in items}
    for j in blocks:
        for i in chunks:
            rs = slice(i * DN_CHUNK, (i + 1) * DN_CHUNK)
            last = (i + 1) * DN_CHUNK - 1
            for bh_ in chains:
                c = bh_ + (j,)
                both = _dot(wq_b[c][i], state[bh_].astype(BF16))
                vn_b = (u[c][rs] - both[:DN_CHUNK]).astype(BF16)
                o_inter[c].append(both[DN_CHUNK:])
                kd_b = (kf[c][rs] * jnp.exp(gh[c][last:last + 1, :] - gh[c][rs])).astype(BF16)
                state[bh_] = state[bh_] * egh[c][last:last + 1, :] + _dot_tn(kd_b, vn_b)
                v_new[c].append(vn_b)
    for bh_ in chains:
        s_ref[bh_[0] * DN_HEADS + bh_[1]] = state[bh_]
    for c in items:
        b, h, j = c
        cs = slice(h * hd, (h + 1) * hd)
        o_h = (jnp.concatenate(o_inter[c], axis=0)
               + _dot_tn(qk_tb[c], jnp.concatenate(v_new[c], axis=0)))
        o_h = o_h * lax.rsqrt(jnp.mean(o_h * o_h, axis=-1, keepdims=True) + EPS) * ng_ref[...]
        gated = o_h * sz_ref[b, j * blk:(j + 1) * blk, cs].astype(F32)
        ob_ref[b * tm + j * blk:b * tm + (j + 1) * blk, cs] = gated.astype(ob_ref.dtype)

    oa = jnp.concatenate([oa_ref[b] for b in range(bsz)], axis=0)
    gate = jnp.concatenate([gate_ref[b] for b in range(bsz)], axis=0).astype(F32)
    merged = gate[:, :d] * _dot(oa, wb0_ref[...]) + gate[:, d:] * _dot(ob_ref[...], wb1_ref[...])
    y = _dot(merged.astype(BF16), wo_ref[...])
    for b in range(bsz):
        ga = mod_ref[b, :, 5 * d:6 * d]
        o_ref[b] = x_ref[b] + ga * y[b * tm:(b + 1) * tm]


def _dn_out(qkv, gb, oa, sz, gate, x, mod3, ng, wb0, wb1, wo, *, tm):
    bsz, seq, d = x.shape
    dnw = sz.shape[-1]

    def tok(n):
        return pl.BlockSpec((bsz, tm, n), lambda i: (0, i, 0))

    in_specs = [tok(3 * dnw), tok(2 * LANES), tok(oa.shape[-1]), tok(dnw), tok(2 * d), tok(d),
                _const_spec(mod3.shape),
                _const_spec(ng.shape), _const_spec(wb0.shape), _const_spec(wb1.shape),
                _const_spec(wo.shape)]
    return pl.pallas_call(
        _dn_out_kernel,
        out_shape=jax.ShapeDtypeStruct((bsz, seq, d), F32),
        grid=(seq // tm,),
        in_specs=in_specs,
        out_specs=tok(d),
        scratch_shapes=[pltpu.VMEM((bsz * DN_HEADS, DN_HEAD_DIM, DN_HEAD_DIM), F32),
                        pltpu.VMEM((bsz * tm, dnw), BF16)],
        compiler_params=pltpu.CompilerParams(
            dimension_semantics=("arbitrary",),
            vmem_limit_bytes=VMEM_LIMIT_BYTES),
        name="dn_out",
    )(qkv, gb, oa, sz, gate, x, mod3, ng, wb0, wb1, wo)


def _pad_lanes(w):
    pad = LANES - w.shape[-1]
    return jnp.pad(w, [(0, 0)] * (w.ndim - 1) + [(0, pad)])


def kernel(x, c, w_ada, b_ada, norm1_g, ffn1_w_in, ffn1_w_out, norm2_g, w_in, conv_w, a_log,
           dt_bias, dn_norm_g, gm_ln_g, gm_ln_b, gm_w_s, gm_b_s, w_branch, w_out, norm3_g,
           ffn2_w_in, ffn2_w_out, final_g):
    bsz, seq, d = x.shape
    depth = w_ada.shape[0]
    assert depth == 1, "the final rmsnorm is fused into the last layer's second FFN"
    width = gm_ln_g.shape[-1]
    dnw = DN_HEADS * DN_HEAD_DIM
    tm_ffn = min(512, seq)
    tm_mix = min(512, seq)
    tm_dn = min(256, seq)

    for l in range(depth):
        mod3 = _ada(c, w_ada, b_ada.reshape(depth, 1, N_MOD * d), l).reshape(bsz, 1, N_MOD * d)
        x, h = _ffn(x, mod3, norm1_g[l], ffn1_w_in[l].astype(BF16), ffn1_w_out[l].astype(BF16),
                    norm2_g[l], k=0, post_k=1, tm=tm_ffn)

        wl = w_in[l]
        o_a = 2 * width + 4 * dnw
        w_main = wl[:, :o_a].astype(BF16)
        w_tail = jnp.concatenate(
            [_pad_lanes(wl[:, o_a:o_a + DN_HEADS]),
             _pad_lanes(wl[:, o_a + DN_HEADS:o_a + 2 * DN_HEADS]), wl[:, o_a + 2 * DN_HEADS:]],
            axis=1).astype(BF16)
        bsf = jnp.repeat(gm_b_s[l].T, width // GM_GROUPS, axis=1)
        oa, qkv, sz, gate, gb = _mix_in(
            h, w_main, w_tail, conv_w[l],
            _pad_lanes(a_log[l].reshape(1, -1)), _pad_lanes(dt_bias[l].reshape(1, -1)),
            gm_ln_g[l].reshape(1, -1), gm_ln_b[l].reshape(1, -1), gm_w_s[l], bsf, tm=tm_mix)
        x = _dn_out(qkv, gb, oa, sz, gate, x, mod3, dn_norm_g[l].reshape(1, -1),
                    w_branch[l, 0].astype(BF16), w_branch[l, 1].astype(BF16),
                    w_out[l].astype(BF16), tm=tm_dn)

        x = _ffn(x, mod3, norm3_g[l], ffn2_w_in[l].astype(BF16), ffn2_w_out[l].astype(BF16),
                 final_g, k=2, post_k=None, tm=tm_ffn)
    return x
```

```python
import functools

import jax
import jax.numpy as jnp
from jax import lax
from jax.experimental import pallas as pl
from jax.experimental.pallas import tpu as pltpu

F32 = jnp.float32
BF16 = jnp.bfloat16

EPS = 1e-6
N_MOD = 9
LANES = 128
GM_CHUNK = 128
GM_GROUPS = 8
DN_HEADS = 8
DN_HEAD_DIM = 128
DN_CONV = 4
DN_CHUNK = 64
DN_BLOCK = 128
CONV_HALO = 8
VMEM_LIMIT_BYTES = 56 * 1024 * 1024


def _dot(a, b):
    return jnp.dot(a, b, preferred_element_type=F32)


def _dot_nt(a, b):
    return lax.dot_general(a, b, (((1,), (1,)), ((), ())), preferred_element_type=F32)


def _dot_tn(a, b):
    return lax.dot_general(a, b, (((0,), (0,)), ((), ())), preferred_element_type=F32)


def _split3(x):
    hi = x.astype(BF16)
    r1 = x - hi.astype(F32)
    mid = r1.astype(BF16)
    lo = (r1 - mid.astype(F32)).astype(BF16)
    return hi, mid, lo


def _dot_sel_left(sel, x):
    hi, mid, lo = _split3(x)
    return _dot(sel, hi) + _dot(sel, mid) + _dot(sel, lo)


def _sigmoid(x):
    return jax.nn.sigmoid(x)


def _silu(x):
    return x * _sigmoid(x)


def _gelu_exact(x):
    return 0.5 * x * (1.0 + lax.erf(x * (2.0 ** -0.5)))


def _rms(x, g):
    return x * lax.rsqrt(jnp.mean(x * x, axis=-1, keepdims=True) + EPS) * g


def _mod_slices(mod_ref, k, d):
    sh = mod_ref[:, (3 * k) * d:(3 * k + 1) * d]
    sc = mod_ref[:, (3 * k + 1) * d:(3 * k + 2) * d]
    ga = mod_ref[:, (3 * k + 2) * d:(3 * k + 3) * d]
    return sh, sc, ga


def _ada_kernel(c_ref, w_ref, b_ref, o_ref):
    ca = _silu(c_ref[...]).astype(BF16)
    o_ref[...] = _dot(ca, w_ref[...].astype(BF16)) + b_ref[...]


def _ada(c, w_ada, b_ada, layer):
    bsz, d = c.shape
    n = w_ada.shape[-1]
    tn = 1024
    c8 = jnp.zeros((8, d), F32).at[:bsz].set(c)
    out = pl.pallas_call(
        _ada_kernel,
        out_shape=jax.ShapeDtypeStruct((8, n), F32),
        grid=(n // tn,),
        in_specs=[pl.BlockSpec((8, d), lambda j: (0, 0)),
                  pl.BlockSpec((None, d, tn), lambda j: (layer, 0, j)),
                  pl.BlockSpec((None, 1, tn), lambda j: (layer, 0, j))],
        out_specs=pl.BlockSpec((8, tn), lambda j: (0, j)),
        compiler_params=pltpu.CompilerParams(
            dimension_semantics=("arbitrary",), vmem_limit_bytes=VMEM_LIMIT_BYTES),
        name="ada_mod",
    )(c8, w_ada, b_ada)
    return out[:bsz]


def _modulated_norm(x, g, mod_ref, k):
    sh, sc, _ = _mod_slices(mod_ref, k, x.shape[-1])
    return (_rms(x, g) * (1.0 + sc) + sh).astype(BF16)


def _ffn_kernel(x_ref, mod_ref, g_ref, win_ref, wout_ref, post_g_ref, *out_refs, k, post_k):
    d = x_ref.shape[-1]
    f = wout_ref.shape[0]
    x = x_ref[...]
    hb = _modulated_norm(x, g_ref[...], mod_ref, k)
    a = _dot(hb, win_ref[:, :f])
    b = _dot(hb, win_ref[:, f:])
    act = (_silu(a) * b).astype(BF16)
    ga = mod_ref[:, (3 * k + 2) * d:(3 * k + 3) * d]
    y = x + (0.5 * ga) * _dot(act, wout_ref[...])
    if post_k is None:
        out_refs[0][...] = _rms(y, post_g_ref[...])
    else:
        out_refs[0][...] = y
        out_refs[1][...] = _modulated_norm(y, post_g_ref[...], mod_ref, post_k)


def _const_spec(shape):
    nd = len(shape)
    return pl.BlockSpec(shape, lambda *_: (0,) * nd, pipeline_mode=pl.Buffered(1))


def _ffn(x, mod3, norm_g, w_in, w_out, post_g, *, k, post_k, tm):
    bsz, seq, d = x.shape
    f = w_out.shape[0]
    tok = pl.BlockSpec((None, tm, d), lambda b, i: (b, i, 0))
    in_specs = [tok,
                pl.BlockSpec((None, 1, N_MOD * d), lambda b, i: (b, 0, 0)),
                _const_spec((1, d)),
                _const_spec((d, 2 * f)),
                _const_spec((f, d)),
                _const_spec((1, d))]
    x_out = jax.ShapeDtypeStruct((bsz, seq, d), F32)
    if post_k is None:
        out_shape, out_specs = x_out, tok
    else:
        out_shape = (x_out, jax.ShapeDtypeStruct((bsz, seq, d), BF16))
        out_specs = (tok, tok)
    return pl.pallas_call(
        functools.partial(_ffn_kernel, k=k, post_k=post_k),
        out_shape=out_shape,
        grid=(bsz, seq // tm),
        in_specs=in_specs,
        out_specs=out_specs,
        compiler_params=pltpu.CompilerParams(
            dimension_semantics=("parallel", "parallel"),
            vmem_limit_bytes=VMEM_LIMIT_BYTES),
        name="ffn_final" if post_k is None else "ffn",
    )(x, mod3, norm_g.reshape(1, d), w_in, w_out, post_g.reshape(1, d))


def _causal_conv(pre, carry, convw_ref):
    w_taps = [convw_ref[DN_CONV - 1 - s:DN_CONV - s, :] for s in range(DN_CONV)]
    groups = [pre[r:r + CONV_HALO] for r in range(0, pre.shape[0], CONV_HALO)]
    out = [g * w_taps[0] for g in groups]
    sub = lax.broadcasted_iota(jnp.int32, carry.shape, 0)
    for s in range(1, DN_CONV):
        rot_prev = pltpu.roll(carry, s, 0)
        for r, g in enumerate(groups):
            rot = pltpu.roll(g, s, 0)
            out[r] = out[r] + jnp.where(sub < s, rot_prev, rot) * w_taps[s]
            rot_prev = rot
    return jnp.concatenate(out, axis=0)


def _mix_in_kernel(h_ref, w_ref, convw_ref, par_ref, ws_ref, bsf_ref,
                   oa_ref, qkv_ref, sz_ref, gate_ref, gb_ref, cbuf_ref, *, n_sub):
    tm = h_ref.shape[0]
    width = oa_ref.shape[-1]
    dnw = sz_ref.shape[-1]
    ts = tm // n_sub
    gd = width // GM_GROUPS
    o_qkv = 2 * width
    o_z = o_qkv + 3 * dnw
    o_ab = o_z + dnw
    o_gate = o_ab + 2 * LANES
    lng, lnb = par_ref[0:1, :], par_ref[1:2, :]
    a_log, dt_bias = par_ref[2:3, :LANES], par_ref[3:4, :LANES]

    @pl.when(pl.program_id(1) == 0)
    def _():
        cbuf_ref[...] = jnp.zeros(cbuf_ref.shape, F32)

    row = lax.broadcasted_iota(jnp.int32, (GM_CHUNK, GM_CHUNK), 0)
    col = lax.broadcasted_iota(jnp.int32, (GM_CHUNK, GM_CHUNK), 1)
    tril = row >= col
    w_mix = [jnp.where(tril, ws_ref[g], 0.0).astype(BF16) for g in range(GM_GROUPS)]
    carry = cbuf_ref[...]

    for r0 in range(0, tm, ts):
        rows = slice(r0, r0 + ts)
        hb = h_ref[rows, :]

        uv = _dot(hb, w_ref[:, :o_qkv])
        pre = _dot(hb, w_ref[:, o_qkv:o_z])
        z = _dot(hb, w_ref[:, o_z:o_ab])
        ab = _dot(hb, w_ref[:, o_ab:o_gate])
        gate_logits = _dot(hb, w_ref[:, o_gate:])

        u = _gelu_exact(uv[:, :width])
        v = _gelu_exact(uv[:, width:])
        mu = jnp.mean(v, axis=-1, keepdims=True)
        vc = v - mu
        var = jnp.mean(vc * vc, axis=-1, keepdims=True)
        vb = (vc * lax.rsqrt(var + EPS) * lng + lnb).astype(BF16)
        for g in range(GM_GROUPS):
            cs = slice(g * gd, (g + 1) * gd)
            for c0 in range(0, ts, 2 * GM_CHUNK):
                pair = jnp.concatenate([vb[c0:c0 + GM_CHUNK, cs],
                                        vb[c0 + GM_CHUNK:c0 + 2 * GM_CHUNK, cs]], axis=1)
                mixed = _dot(w_mix[g], pair)
                for j in range(2):
                    cr = slice(c0 + j * GM_CHUNK, c0 + (j + 1) * GM_CHUNK)
                    mix_j = mixed[:, j * gd:(j + 1) * gd] + bsf_ref[:, cs]
                    oa_ref[r0 + cr.start:r0 + cr.stop, cs] = (
                        u[cr, cs] * mix_j).astype(oa_ref.dtype)

        act = _silu(_causal_conv(pre, carry, convw_ref))
        carry = pre[ts - CONV_HALO:ts, :]
        for h in range(DN_HEADS):
            for part in range(2):
                cs = slice(part * dnw + h * DN_HEAD_DIM, part * dnw + (h + 1) * DN_HEAD_DIM)
                t = act[:, cs]
                t = t * lax.rsqrt(jnp.sum(t * t, axis=-1, keepdims=True) + EPS)
                if part == 0:
                    t = t * (DN_HEAD_DIM ** -0.5)
                qkv_ref[rows, cs] = t.astype(qkv_ref.dtype)
        qkv_ref[rows, 2 * dnw:] = act[:, 2 * dnw:].astype(qkv_ref.dtype)

        sz_ref[rows, :] = _silu(z).astype(sz_ref.dtype)
        gate_ref[rows, :] = _sigmoid(gate_logits).astype(gate_ref.dtype)
        gb_ref[rows, :LANES] = -jnp.exp(a_log) * jax.nn.softplus(ab[:, :LANES] + dt_bias)
        gb_ref[rows, LANES:] = _sigmoid(ab[:, LANES:])
    cbuf_ref[...] = carry


def _mix_in(h, w_pack, conv_w, params, w_s, bsf, *, tm):
    bsz, seq, d = h.shape
    width = params.shape[-1]
    dnw = conv_w.shape[-1] // 3

    def tok(n):
        return pl.BlockSpec((None, tm, n), lambda b, i: (b, i, 0))

    in_specs = [tok(d),
                _const_spec(w_pack.shape),
                _const_spec(conv_w.shape), _const_spec(params.shape),
                _const_spec(w_s.shape), _const_spec(bsf.shape)]
    out_shape = (jax.ShapeDtypeStruct((bsz, seq, width), BF16),
                 jax.ShapeDtypeStruct((bsz, seq, 3 * dnw), BF16),
                 jax.ShapeDtypeStruct((bsz, seq, dnw), BF16),
                 jax.ShapeDtypeStruct((bsz, seq, 2 * d), BF16),
                 jax.ShapeDtypeStruct((bsz, seq, 2 * LANES), F32))
    out_specs = (tok(width), tok(3 * dnw), tok(dnw), tok(2 * d), tok(2 * LANES))
    return pl.pallas_call(
        functools.partial(_mix_in_kernel, n_sub=2),
        out_shape=out_shape,
        grid=(bsz, seq // tm),
        in_specs=in_specs,
        out_specs=out_specs,
        scratch_shapes=[pltpu.VMEM((CONV_HALO, 3 * dnw), F32)],
        compiler_params=pltpu.CompilerParams(
            dimension_semantics=("arbitrary", "arbitrary"),
            vmem_limit_bytes=VMEM_LIMIT_BYTES),
        name="mix_in",
    )(h, w_pack, conv_w, params, w_s, bsf)


def _dn_out_kernel(qkv_ref, gb_ref, oa_ref, sz_ref, gate_ref, x_ref, mod_ref, ng_ref,
                   wb0_ref, wb1_ref, wo_ref, o_ref, s_ref, ob_ref):
    bsz, tm, d = x_ref.shape
    dnw = sz_ref.shape[-1]
    hd = DN_HEAD_DIM
    blk = DN_BLOCK
    blocks = range(tm // blk)
    chunks = range(blk // DN_CHUNK)
    chains = [(b, h) for b in range(bsz) for h in range(DN_HEADS)]
    items = [(b, h, j) for (b, h) in chains for j in blocks]

    @pl.when(pl.program_id(0) == 0)
    def _():
        s_ref[...] = jnp.zeros(s_ref.shape, F32)

    row = lax.broadcasted_iota(jnp.int32, (blk, blk), 0)
    col = lax.broadcasted_iota(jnp.int32, (blk, blk), 1)
    shift = DN_CHUNK.bit_length() - 1
    same = jnp.right_shift(row, shift) == jnp.right_shift(col, shift)
    cum_sel = jnp.where(jnp.logical_and(same, col <= row), 1.0, 0.0).astype(BF16)
    upper = jnp.logical_and(same, row <= col)
    strict_upper = jnp.logical_and(same, row < col)
    e_row = lax.broadcasted_iota(jnp.int32, (2 * LANES, dnw), 0)
    e_col = lax.broadcasted_iota(jnp.int32, (2 * LANES, dnw), 1)
    expand = jnp.where(jnp.right_shift(e_col, hd.bit_length() - 1)
                       == jnp.bitwise_and(e_row, LANES - 1), 1.0, 0.0).astype(BF16)

    def lane_expand(x):
        hi = x.astype(BF16)
        mid = (x - hi.astype(F32)).astype(BF16)
        return _dot(jnp.concatenate([hi, mid], axis=1), expand)

    g_full, b_full, gc_t, eg_full = [], [], [], []
    for b in range(bsz):
        gc = [_dot_sel_left(cum_sel, gb_ref[b, j * blk:(j + 1) * blk, :LANES]) for j in blocks]
        g_full.append(lane_expand(jnp.concatenate(gc, axis=0)))
        b_full.append(lane_expand(gb_ref[b, :, LANES:]))
        gc_t.append([g.T for g in gc])
        eg_full.append(jnp.exp(g_full[b]))

    k_b, kf, gh, egh, decay_t, rhs, qd_b = {}, {}, {}, {}, {}, {}, {}
    kk_t, qk_tb = {}, {}
    for c in items:
        b, h, j = c
        rows = slice(j * blk, (j + 1) * blk)
        cs = slice(h * hd, (h + 1) * hd)
        q = qkv_ref[b, rows, h * hd:(h + 1) * hd]
        k_b[c] = qkv_ref[b, rows, dnw + h * hd:dnw + (h + 1) * hd]
        v = qkv_ref[b, rows, 2 * dnw + h * hd:2 * dnw + (h + 1) * hd]
        kf[c] = k_b[c].astype(F32)
        gh[c] = g_full[b][rows, cs]
        egh[c] = eg_full[b][rows, cs]
        bh = b_full[b][rows, cs]
        kbeta = kf[c] * bh
        diff_t = gc_t[b][j][h:h + 1, :] - gh[c]
        decay_t[c] = jnp.where(upper, jnp.exp(jnp.where(upper, diff_t, 0.0)), 0.0)
        rhs[c] = jnp.concatenate([v.astype(F32) * bh, kbeta * egh[c]], axis=1)
        qd_b[c] = (q.astype(F32) * egh[c]).astype(BF16)
        both_t = _dot_nt(k_b[c], jnp.concatenate([kbeta.astype(BF16), q], axis=0))
        kk_t[c] = both_t[:, :blk]
        qk_tb[c] = (both_t[:, blk:] * decay_t[c]).astype(BF16)

    n, x = {}, {}
    for c in items:
        a_up = jnp.where(strict_upper, kk_t[c] * decay_t[c], 0.0)
        a_b = a_up.astype(BF16)
        n[c] = -a_up
        x[c] = _dot(a_b, a_b)
    p = 2
    while p < DN_CHUNK:
        for c in items:
            x_b = x[c].astype(BF16)
            n_b = n[c].astype(BF16)
            if 2 * p < DN_CHUNK:
                both = _dot(x_b, jnp.concatenate([x_b, n_b], axis=1))
                x_next, xn = both[:, :blk], both[:, blk:]
            else:
                x_next, xn = None, _dot(x_b, n_b)
            n[c] = n[c] + x[c] + xn
            x[c] = x_next
        p *= 2

    u, wq_b = {}, {}
    for c in items:
        sol = rhs[c] + _dot_tn(n[c].astype(BF16), rhs[c].astype(BF16))
        u[c] = sol[:, :hd]
        w_b = sol[:, hd:].astype(BF16)
        wq_b[c] = [jnp.concatenate([w_b[i * DN_CHUNK:(i + 1) * DN_CHUNK],
                                    qd_b[c][i * DN_CHUNK:(i + 1) * DN_CHUNK]], axis=0)
                   for i in chunks]

    state = {bh_: s_ref[bh_[0] * DN_HEADS + bh_[1]] for bh_ in chains}
    o_inter = {c: [] for c in items}
    v_new = {c: [] for c in items}
    for j in blocks:
        for i in chunks:
            rs = slice(i * DN_CHUNK, (i + 1) * DN_CHUNK)
            last = (i + 1) * DN_CHUNK - 1
            for bh_ in chains:
                c = bh_ + (j,)
                both = _dot(wq_b[c][i], state[bh_].astype(BF16))
                vn_b = (u[c][rs] - both[:DN_CHUNK]).astype(BF16)
                o_inter[c].append(both[DN_CHUNK:])
                kd_b = (kf[c][rs] * jnp.exp(gh[c][last:last + 1, :] - gh[c][rs])).astype(BF16)
                state[bh_] = state[bh_] * egh[c][last:last + 1, :] + _dot_tn(kd_b, vn_b)
                v_new[c].append(vn_b)
    for bh_ in chains:
        s_ref[bh_[0] * DN_HEADS + bh_[1]] = state[bh_]
    for c in items:
        b, h, j = c
        cs = slice(h * hd, (h + 1) * hd)
        o_h = (jnp.concatenate(o_inter[c], axis=0)
               + _dot_tn(qk_tb[c], jnp.concatenate(v_new[c], axis=0)))
        o_h = o_h * lax.rsqrt(jnp.mean(o_h * o_h, axis=-1, keepdims=True) + EPS) * ng_ref[...]
        gated = o_h * sz_ref[b, j * blk:(j + 1) * blk, cs].astype(F32)
        ob_ref[b * tm + j * blk:b * tm + (j + 1) * blk, cs] = gated.astype(ob_ref.dtype)

    oa = jnp.concatenate([oa_ref[b] for b in range(bsz)], axis=0)
    gate = jnp.concatenate([gate_ref[b] for b in range(bsz)], axis=0).astype(F32)
    merged = gate[:, :d] * _dot(oa, wb0_ref[...]) + gate[:, d:] * _dot(ob_ref[...], wb1_ref[...])
    y = _dot(merged.astype(BF16), wo_ref[...])
    for b in range(bsz):
        ga = mod_ref[b, :, 5 * d:6 * d]
        o_ref[b] = x_ref[b] + ga * y[b * tm:(b + 1) * tm]


def _dn_out(qkv, gb, oa, sz, gate, x, mod3, ng, wb0, wb1, wo, *, tm):
    bsz, seq, d = x.shape
    dnw = sz.shape[-1]

    def tok(n):
        return pl.BlockSpec((bsz, tm, n), lambda i: (0, i, 0))

    in_specs = [tok(3 * dnw), tok(2 * LANES), tok(oa.shape[-1]), tok(dnw), tok(2 * d), tok(d),
                _const_spec(mod3.shape),
                _const_spec(ng.shape), _const_spec(wb0.shape), _const_spec(wb1.shape),
                _const_spec(wo.shape)]
    return pl.pallas_call(
        _dn_out_kernel,
        out_shape=jax.ShapeDtypeStruct((bsz, seq, d), F32),
        grid=(seq // tm,),
        in_specs=in_specs,
        out_specs=tok(d),
        scratch_shapes=[pltpu.VMEM((bsz * DN_HEADS, DN_HEAD_DIM, DN_HEAD_DIM), F32),
                        pltpu.VMEM((bsz * tm, dnw), BF16)],
        compiler_params=pltpu.CompilerParams(
            dimension_semantics=("arbitrary",),
            vmem_limit_bytes=VMEM_LIMIT_BYTES),
        name="dn_out",
    )(qkv, gb, oa, sz, gate, x, mod3, ng, wb0, wb1, wo)


def _pad_lanes(w):
    pad = LANES - w.shape[-1]
    return jnp.pad(w, [(0, 0)] * (w.ndim - 1) + [(0, pad)])


def kernel(x, c, w_ada, b_ada, norm1_g, ffn1_w_in, ffn1_w_out, norm2_g, w_in, conv_w, a_log,
           dt_bias, dn_norm_g, gm_ln_g, gm_ln_b, gm_w_s, gm_b_s, w_branch, w_out, norm3_g,
           ffn2_w_in, ffn2_w_out, final_g):
    bsz, seq, d = x.shape
    depth = w_ada.shape[0]
    assert depth == 1, "the final rmsnorm is fused into the last layer's second FFN"
    width = gm_ln_g.shape[-1]
    dnw = DN_HEADS * DN_HEAD_DIM
    tm_ffn = min(512, seq)
    tm_mix = min(512, seq)
    tm_dn = min(256, seq)

    for l in range(depth):
        mod3 = _ada(c, w_ada, b_ada.reshape(depth, 1, N_MOD * d), l).reshape(bsz, 1, N_MOD * d)
        x, h = _ffn(x, mod3, norm1_g[l], ffn1_w_in[l].astype(BF16), ffn1_w_out[l].astype(BF16),
                    norm2_g[l], k=0, post_k=1, tm=tm_ffn)

        wl = w_in[l]
        o_a = 2 * width + 4 * dnw
        w_pack = jnp.concatenate(
            [wl[:, :o_a], _pad_lanes(wl[:, o_a:o_a + DN_HEADS]),
             _pad_lanes(wl[:, o_a + DN_HEADS:o_a + 2 * DN_HEADS]), wl[:, o_a + 2 * DN_HEADS:]],
            axis=1).astype(BF16)
        params = (jnp.zeros((8, width), F32).at[0].set(gm_ln_g[l]).at[1].set(gm_ln_b[l])
                  .at[2, :DN_HEADS].set(a_log[l]).at[3, :DN_HEADS].set(dt_bias[l]))
        bsf = jnp.repeat(gm_b_s[l].T, width // GM_GROUPS, axis=1)
        oa, qkv, sz, gate, gb = _mix_in(
            h, w_pack, conv_w[l], params, gm_w_s[l], bsf, tm=tm_mix)
        x = _dn_out(qkv, gb, oa, sz, gate, x, mod3, dn_norm_g[l].reshape(1, -1),
                    w_branch[l, 0].astype(BF16), w_branch[l, 1].astype(BF16),
                    w_out[l].astype(BF16), tm=tm_dn)

        x = _ffn(x, mod3, norm3_g[l], ffn2_w_in[l].astype(BF16), ffn2_w_out[l].astype(BF16),
                 final_g, k=2, post_k=None, tm=tm_ffn)
    return x
```

```python
import functools

import jax
import jax.numpy as jnp
from jax import lax
from jax.experimental import pallas as pl
from jax.experimental.pallas import tpu as pltpu

F32 = jnp.float32
BF16 = jnp.bfloat16

EPS = 1e-6
N_MOD = 9
LANES = 128
GM_CHUNK = 128
GM_GROUPS = 8
DN_HEADS = 8
DN_HEAD_DIM = 128
DN_CONV = 4
DN_CHUNK = 64
DN_BLOCK = 128
CONV_HALO = 8
FFN_SUB_ROWS = 512
VMEM_LIMIT_BYTES = 56 * 1024 * 1024


def _dot(a, b):
    return jnp.dot(a, b, preferred_element_type=F32)


def _dot_nt(a, b):
    return lax.dot_general(a, b, (((1,), (1,)), ((), ())), preferred_element_type=F32)


def _dot_tn(a, b):
    return lax.dot_general(a, b, (((0,), (0,)), ((), ())), preferred_element_type=F32)


def _split3(x):
    hi = x.astype(BF16)
    r1 = x - hi.astype(F32)
    mid = r1.astype(BF16)
    lo = (r1 - mid.astype(F32)).astype(BF16)
    return hi, mid, lo


def _dot_sel_left(sel, x):
    hi, mid, lo = _split3(x)
    return _dot(sel, hi) + _dot(sel, mid) + _dot(sel, lo)


def _sigmoid(x):
    return jax.nn.sigmoid(x)


def _silu(x):
    return x * _sigmoid(x)


def _gelu_exact(x):
    return 0.5 * x * (1.0 + lax.erf(x * (2.0 ** -0.5)))


def _rms(x, g):
    return x * lax.rsqrt(jnp.mean(x * x, axis=-1, keepdims=True) + EPS) * g


def _mod_slices(mod_ref, k, d):
    sh = mod_ref[:, (3 * k) * d:(3 * k + 1) * d]
    sc = mod_ref[:, (3 * k + 1) * d:(3 * k + 2) * d]
    ga = mod_ref[:, (3 * k + 2) * d:(3 * k + 3) * d]
    return sh, sc, ga


def _ada_kernel(c_ref, w_ref, b_ref, o_ref):
    ca = _silu(c_ref[...]).astype(BF16)
    o_ref[...] = _dot(ca, w_ref[...].astype(BF16)) + b_ref[...]


def _ada(c, w_ada, b_ada, layer):
    bsz, d = c.shape
    n = w_ada.shape[-1]
    tn = 1024
    c8 = jnp.zeros((8, d), F32).at[:bsz].set(c)
    out = pl.pallas_call(
        _ada_kernel,
        out_shape=jax.ShapeDtypeStruct((8, n), F32),
        grid=(n // tn,),
        in_specs=[pl.BlockSpec((8, d), lambda j: (0, 0)),
                  pl.BlockSpec((None, d, tn), lambda j: (layer, 0, j)),
                  pl.BlockSpec((None, 1, tn), lambda j: (layer, 0, j))],
        out_specs=pl.BlockSpec((8, tn), lambda j: (0, j)),
        compiler_params=pltpu.CompilerParams(
            dimension_semantics=("arbitrary",), vmem_limit_bytes=VMEM_LIMIT_BYTES),
        name="ada_mod",
    )(c8, w_ada, b_ada)
    return out[:bsz]


def _modulated_norm(x, g, mod_ref, k):
    sh, sc, _ = _mod_slices(mod_ref, k, x.shape[-1])
    return (_rms(x, g) * (1.0 + sc) + sh).astype(BF16)


def _ffn_kernel(x_ref, mod_ref, g_ref, win_ref, wout_ref, post_g_ref, *out_refs, k, post_k):
    tm, d = x_ref.shape
    f = wout_ref.shape[0]
    ga = mod_ref[:, (3 * k + 2) * d:(3 * k + 3) * d]
    for r0 in range(0, tm, FFN_SUB_ROWS):
        rows = slice(r0, r0 + FFN_SUB_ROWS)
        x = x_ref[rows, :]
        hb = _modulated_norm(x, g_ref[...], mod_ref, k)
        a = _dot(hb, win_ref[:, :f])
        b = _dot(hb, win_ref[:, f:])
        act = (_silu(a) * b).astype(BF16)
        y = x + (0.5 * ga) * _dot(act, wout_ref[...])
        if post_k is None:
            out_refs[0][rows, :] = _rms(y, post_g_ref[...])
        else:
            out_refs[0][rows, :] = y
            out_refs[1][rows, :] = _modulated_norm(y, post_g_ref[...], mod_ref, post_k)


def _const_spec(shape):
    nd = len(shape)
    return pl.BlockSpec(shape, lambda *_: (0,) * nd, pipeline_mode=pl.Buffered(1))


def _ffn(x, mod3, norm_g, w_in, w_out, post_g, *, k, post_k, tm):
    bsz, seq, d = x.shape
    f = w_out.shape[0]
    tok = pl.BlockSpec((None, tm, d), lambda b, i: (b, i, 0))
    in_specs = [tok,
                pl.BlockSpec((None, 1, N_MOD * d), lambda b, i: (b, 0, 0)),
                _const_spec((1, d)),
                _const_spec((d, 2 * f)),
                _const_spec((f, d)),
                _const_spec((1, d))]
    x_out = jax.ShapeDtypeStruct((bsz, seq, d), F32)
    if post_k is None:
        out_shape, out_specs = x_out, tok
    else:
        out_shape = (x_out, jax.ShapeDtypeStruct((bsz, seq, d), BF16))
        out_specs = (tok, tok)
    return pl.pallas_call(
        functools.partial(_ffn_kernel, k=k, post_k=post_k),
        out_shape=out_shape,
        grid=(bsz, seq // tm),
        in_specs=in_specs,
        out_specs=out_specs,
        compiler_params=pltpu.CompilerParams(
            dimension_semantics=("parallel", "parallel"),
            vmem_limit_bytes=VMEM_LIMIT_BYTES),
        name="ffn_final" if post_k is None else "ffn",
    )(x, mod3, norm_g.reshape(1, d), w_in, w_out, post_g.reshape(1, d))


def _causal_conv(pre, carry, convw_ref):
    w_taps = [convw_ref[DN_CONV - 1 - s:DN_CONV - s, :] for s in range(DN_CONV)]
    groups = [pre[r:r + CONV_HALO] for r in range(0, pre.shape[0], CONV_HALO)]
    out = [g * w_taps[0] for g in groups]
    sub = lax.broadcasted_iota(jnp.int32, carry.shape, 0)
    for s in range(1, DN_CONV):
        rot_prev = pltpu.roll(carry, s, 0)
        for r, g in enumerate(groups):
            rot = pltpu.roll(g, s, 0)
            out[r] = out[r] + jnp.where(sub < s, rot_prev, rot) * w_taps[s]
            rot_prev = rot
    return jnp.concatenate(out, axis=0)


def _mix_in_kernel(h_ref, w_ref, convw_ref, par_ref, ws_ref, bsf_ref,
                   oa_ref, qkv_ref, sz_ref, gate_ref, gb_ref, cbuf_ref, *, n_sub):
    tm = h_ref.shape[0]
    width = oa_ref.shape[-1]
    dnw = sz_ref.shape[-1]
    ts = tm // n_sub
    gd = width // GM_GROUPS
    o_qkv = 2 * width
    o_z = o_qkv + 3 * dnw
    o_ab = o_z + dnw
    o_gate = o_ab + 2 * LANES
    lng, lnb = par_ref[0:1, :], par_ref[1:2, :]
    a_log, dt_bias = par_ref[2:3, :LANES], par_ref[3:4, :LANES]

    @pl.when(pl.program_id(1) == 0)
    def _():
        cbuf_ref[...] = jnp.zeros(cbuf_ref.shape, F32)

    row = lax.broadcasted_iota(jnp.int32, (GM_CHUNK, GM_CHUNK), 0)
    col = lax.broadcasted_iota(jnp.int32, (GM_CHUNK, GM_CHUNK), 1)
    tril = row >= col
    w_mix = [jnp.where(tril, ws_ref[g], 0.0).astype(BF16) for g in range(GM_GROUPS)]
    carry = cbuf_ref[...]

    for r0 in range(0, tm, ts):
        rows = slice(r0, r0 + ts)
        hb = h_ref[rows, :]

        uv = _dot(hb, w_ref[:, :o_qkv])
        pre = _dot(hb, w_ref[:, o_qkv:o_z])
        z = _dot(hb, w_ref[:, o_z:o_ab])
        ab = _dot(hb, w_ref[:, o_ab:o_gate])
        gate_logits = _dot(hb, w_ref[:, o_gate:])

        u = _gelu_exact(uv[:, :width])
        v = _gelu_exact(uv[:, width:])
        mu = jnp.mean(v, axis=-1, keepdims=True)
        vc = v - mu
        var = jnp.mean(vc * vc, axis=-1, keepdims=True)
        vb = (vc * lax.rsqrt(var + EPS) * lng + lnb).astype(BF16)
        for g in range(GM_GROUPS):
            cs = slice(g * gd, (g + 1) * gd)
            for c0 in range(0, ts, 2 * GM_CHUNK):
                pair = jnp.concatenate([vb[c0:c0 + GM_CHUNK, cs],
                                        vb[c0 + GM_CHUNK:c0 + 2 * GM_CHUNK, cs]], axis=1)
                mixed = _dot(w_mix[g], pair)
                for j in range(2):
                    cr = slice(c0 + j * GM_CHUNK, c0 + (j + 1) * GM_CHUNK)
                    mix_j = mixed[:, j * gd:(j + 1) * gd] + bsf_ref[:, cs]
                    oa_ref[r0 + cr.start:r0 + cr.stop, cs] = (
                        u[cr, cs] * mix_j).astype(oa_ref.dtype)

        act = _silu(_causal_conv(pre, carry, convw_ref))
        carry = pre[ts - CONV_HALO:ts, :]
        for h in range(DN_HEADS):
            for part in range(2):
                cs = slice(part * dnw + h * DN_HEAD_DIM, part * dnw + (h + 1) * DN_HEAD_DIM)
                t = act[:, cs]
                t = t * lax.rsqrt(jnp.sum(t * t, axis=-1, keepdims=True) + EPS)
                if part == 0:
                    t = t * (DN_HEAD_DIM ** -0.5)
                qkv_ref[rows, cs] = t.astype(qkv_ref.dtype)
        qkv_ref[rows, 2 * dnw:] = act[:, 2 * dnw:].astype(qkv_ref.dtype)

        sz_ref[rows, :] = _silu(z).astype(sz_ref.dtype)
        gate_ref[rows, :] = _sigmoid(gate_logits).astype(gate_ref.dtype)
        gb_ref[rows, :LANES] = -jnp.exp(a_log) * jax.nn.softplus(ab[:, :LANES] + dt_bias)
        gb_ref[rows, LANES:] = _sigmoid(ab[:, LANES:])
    cbuf_ref[...] = carry


def _mix_in(h, w_pack, conv_w, params, w_s, bsf, *, tm):
    bsz, seq, d = h.shape
    width = params.shape[-1]
    dnw = conv_w.shape[-1] // 3

    def tok(n):
        return pl.BlockSpec((None, tm, n), lambda b, i: (b, i, 0))

    in_specs = [tok(d),
                _const_spec(w_pack.shape),
                _const_spec(conv_w.shape), _const_spec(params.shape),
                _const_spec(w_s.shape), _const_spec(bsf.shape)]
    out_shape = (jax.ShapeDtypeStruct((bsz, seq, width), BF16),
                 jax.ShapeDtypeStruct((bsz, seq, 3 * dnw), BF16),
                 jax.ShapeDtypeStruct((bsz, seq, dnw), BF16),
                 jax.ShapeDtypeStruct((bsz, seq, 2 * d), BF16),
                 jax.ShapeDtypeStruct((bsz, seq, 2 * LANES), F32))
    out_specs = (tok(width), tok(3 * dnw), tok(dnw), tok(2 * d), tok(2 * LANES))
    return pl.pallas_call(
        functools.partial(_mix_in_kernel, n_sub=2),
        out_shape=out_shape,
        grid=(bsz, seq // tm),
        in_specs=in_specs,
        out_specs=out_specs,
        scratch_shapes=[pltpu.VMEM((CONV_HALO, 3 * dnw), F32)],
        compiler_params=pltpu.CompilerParams(
            dimension_semantics=("arbitrary", "arbitrary"),
            vmem_limit_bytes=VMEM_LIMIT_BYTES),
        name="mix_in",
    )(h, w_pack, conv_w, params, w_s, bsf)


def _dn_out_kernel(qkv_ref, gb_ref, oa_ref, sz_ref, gate_ref, x_ref, mod_ref, ng_ref,
                   wb0_ref, wb1_ref, wo_ref, o_ref, s_ref, ob_ref):
    bsz, tm, d = x_ref.shape
    dnw = sz_ref.shape[-1]
    hd = DN_HEAD_DIM
    blk = DN_BLOCK
    blocks = range(tm // blk)
    chunks = range(blk // DN_CHUNK)
    chains = [(b, h) for b in range(bsz) for h in range(DN_HEADS)]
    items = [(b, h, j) for (b, h) in chains for j in blocks]

    @pl.when(pl.program_id(0) == 0)
    def _():
        s_ref[...] = jnp.zeros(s_ref.shape, F32)

    row = lax.broadcasted_iota(jnp.int32, (blk, blk), 0)
    col = lax.broadcasted_iota(jnp.int32, (blk, blk), 1)
    shift = DN_CHUNK.bit_length() - 1
    same = jnp.right_shift(row, shift) == jnp.right_shift(col, shift)
    cum_sel = jnp.where(jnp.logical_and(same, col <= row), 1.0, 0.0).astype(BF16)
    upper = jnp.logical_and(same, row <= col)
    strict_upper = jnp.logical_and(same, row < col)
    e_row = lax.broadcasted_iota(jnp.int32, (2 * LANES, dnw), 0)
    e_col = lax.broadcasted_iota(jnp.int32, (2 * LANES, dnw), 1)
    expand = jnp.where(jnp.right_shift(e_col, hd.bit_length() - 1)
                       == jnp.bitwise_and(e_row, LANES - 1), 1.0, 0.0).astype(BF16)

    def lane_expand(x):
        hi = x.astype(BF16)
        mid = (x - hi.astype(F32)).astype(BF16)
        return _dot(jnp.concatenate([hi, mid], axis=1), expand)

    g_full, b_full, gc_t, eg_full = [], [], [], []
    for b in range(bsz):
        gc = [_dot_sel_left(cum_sel, gb_ref[b, j * blk:(j + 1) * blk, :LANES]) for j in blocks]
        g_full.append(lane_expand(jnp.concatenate(gc, axis=0)))
        b_full.append(lane_expand(gb_ref[b, :, LANES:]))
        gc_t.append([g.T for g in gc])
        eg_full.append(jnp.exp(g_full[b]))

    k_b, kf, gh, egh, decay_t, rhs, qd_b = {}, {}, {}, {}, {}, {}, {}
    kk_t, qk_tb = {}, {}
    for c in items:
        b, h, j = c
        rows = slice(j * blk, (j + 1) * blk)
        cs = slice(h * hd, (h + 1) * hd)
        q = qkv_ref[b, rows, h * hd:(h + 1) * hd]
        k_b[c] = qkv_ref[b, rows, dnw + h * hd:dnw + (h + 1) * hd]
        v = qkv_ref[b, rows, 2 * dnw + h * hd:2 * dnw + (h + 1) * hd]
        kf[c] = k_b[c].astype(F32)
        gh[c] = g_full[b][rows, cs]
        egh[c] = eg_full[b][rows, cs]
        bh = b_full[b][rows, cs]
        kbeta = kf[c] * bh
        diff_t = gc_t[b][j][h:h + 1, :] - gh[c]
        decay_t[c] = jnp.where(upper, jnp.exp(jnp.where(upper, diff_t, 0.0)), 0.0)
        rhs[c] = jnp.concatenate([v.astype(F32) * bh, kbeta * egh[c]], axis=1)
        qd_b[c] = (q.astype(F32) * egh[c]).astype(BF16)
        both_t = _dot_nt(k_b[c], jnp.concatenate([kbeta.astype(BF16), q], axis=0))
        kk_t[c] = both_t[:, :blk]
        qk_tb[c] = (both_t[:, blk:] * decay_t[c]).astype(BF16)

    n, x = {}, {}
    for c in items:
        a_up = jnp.where(strict_upper, kk_t[c] * decay_t[c], 0.0)
        a_b = a_up.astype(BF16)
        n[c] = -a_up
        x[c] = _dot(a_b, a_b)
    p = 2
    while p < DN_CHUNK:
        for c in items:
            x_b = x[c].astype(BF16)
            n_b = n[c].astype(BF16)
            if 2 * p < DN_CHUNK:
                both = _dot(x_b, jnp.concatenate([x_b, n_b], axis=1))
                x_next, xn = both[:, :blk], both[:, blk:]
            else:
                x_next, xn = None, _dot(x_b, n_b)
            n[c] = n[c] + x[c] + xn
            x[c] = x_next
        p *= 2

    u, wq_b = {}, {}
    for c in items:
        sol = rhs[c] + _dot_tn(n[c].astype(BF16), rhs[c].astype(BF16))
        u[c] = sol[:, :hd]
        w_b = sol[:, hd:].astype(BF16)
        wq_b[c] = [jnp.concatenate([w_b[i * DN_CHUNK:(i + 1) * DN_CHUNK],
                                    qd_b[c][i * DN_CHUNK:(i + 1) * DN_CHUNK]], axis=0)
                   for i in chunks]

    state = {bh_: s_ref[bh_[0] * DN_HEADS + bh_[1]] for bh_ in chains}
    o_inter = {c: [] for c in items}
    v_new = {c: [] for c in items}
    for j in blocks:
        for i in chunks:
            rs = slice(i * DN_CHUNK, (i + 1) * DN_CHUNK)
            last = (i + 1) * DN_CHUNK - 1
            for bh_ in chains:
                c = bh_ + (j,)
                both = _dot(wq_b[c][i], state[bh_].astype(BF16))
                vn_b = (u[c][rs] - both[:DN_CHUNK]).astype(BF16)
                o_inter[c].append(both[DN_CHUNK:])
                kd_b = (kf[c][rs] * jnp.exp(gh[c][last:last + 1, :] - gh[c][rs])).astype(BF16)
                state[bh_] = state[bh_] * egh[c][last:last + 1, :] + _dot_tn(kd_b, vn_b)
                v_new[c].append(vn_b)
    for bh_ in chains:
        s_ref[bh_[0] * DN_HEADS + bh_[1]] = state[bh_]
    for c in items:
        b, h, j = c
        cs = slice(h * hd, (h + 1) * hd)
        o_h = (jnp.concatenate(o_inter[c], axis=0)
               + _dot_tn(qk_tb[c], jnp.concatenate(v_new[c], axis=0)))
        o_h = o_h * lax.rsqrt(jnp.mean(o_h * o_h, axis=-1, keepdims=True) + EPS) * ng_ref[...]
        gated = o_h * sz_ref[b, j * blk:(j + 1) * blk, cs].astype(F32)
        ob_ref[b * tm + j * blk:b * tm + (j + 1) * blk, cs] = gated.astype(ob_ref.dtype)

    oa = jnp.concatenate([oa_ref[b] for b in range(bsz)], axis=0)
    gate = jnp.concatenate([gate_ref[b] for b in range(bsz)], axis=0).astype(F32)
    merged = gate[:, :d] * _dot(oa, wb0_ref[...]) + gate[:, d:] * _dot(ob_ref[...], wb1_ref[...])
    y = _dot(merged.astype(BF16), wo_ref[...])
    for b in range(bsz):
        ga = mod_ref[b, :, 5 * d:6 * d]
        o_ref[b] = x_ref[b] + ga * y[b * tm:(b + 1) * tm]


def _dn_out(qkv, gb, oa, sz, gate, x, mod3, ng, wb0, wb1, wo, *, tm):
    bsz, seq, d = x.shape
    dnw = sz.shape[-1]

    def tok(n):
        return pl.BlockSpec((bsz, tm, n), lambda i: (0, i, 0))

    in_specs = [tok(3 * dnw), tok(2 * LANES), tok(oa.shape[-1]), tok(dnw), tok(2 * d), tok(d),
                _const_spec(mod3.shape),
                _const_spec(ng.shape), _const_spec(wb0.shape), _const_spec(wb1.shape),
                _const_spec(wo.shape)]
    return pl.pallas_call(
        _dn_out_kernel,
        out_shape=jax.ShapeDtypeStruct((bsz, seq, d), F32),
        grid=(seq // tm,),
        in_specs=in_specs,
        out_specs=tok(d),
        scratch_shapes=[pltpu.VMEM((bsz * DN_HEADS, DN_HEAD_DIM, DN_HEAD_DIM), F32),
                        pltpu.VMEM((bsz * tm, dnw), BF16)],
        compiler_params=pltpu.CompilerParams(
            dimension_semantics=("arbitrary",),
            vmem_limit_bytes=VMEM_LIMIT_BYTES),
        name="dn_out",
    )(qkv, gb, oa, sz, gate, x, mod3, ng, wb0, wb1, wo)


def _pad_lanes(w):
    pad = LANES - w.shape[-1]
    return jnp.pad(w, [(0, 0)] * (w.ndim - 1) + [(0, pad)])


def kernel(x, c, w_ada, b_ada, norm1_g, ffn1_w_in, ffn1_w_out, norm2_g, w_in, conv_w, a_log,
           dt_bias, dn_norm_g, gm_ln_g, gm_ln_b, gm_w_s, gm_b_s, w_branch, w_out, norm3_g,
           ffn2_w_in, ffn2_w_out, final_g):
    bsz, seq, d = x.shape
    depth = w_ada.shape[0]
    assert depth == 1, "the final rmsnorm is fused into the last layer's second FFN"
    width = gm_ln_g.shape[-1]
    dnw = DN_HEADS * DN_HEAD_DIM
    tm_ffn = min(1024, seq)
    tm_mix = min(512, seq)
    tm_dn = min(256, seq)

    for l in range(depth):
        mod3 = _ada(c, w_ada, b_ada.reshape(depth, 1, N_MOD * d), l).reshape(bsz, 1, N_MOD * d)
        x, h = _ffn(x, mod3, norm1_g[l], ffn1_w_in[l].astype(BF16), ffn1_w_out[l].astype(BF16),
                    norm2_g[l], k=0, post_k=1, tm=tm_ffn)

        wl = w_in[l]
        o_a = 2 * width + 4 * dnw
        w_pack = jnp.concatenate(
            [wl[:, :o_a], _pad_lanes(wl[:, o_a:o_a + DN_HEADS]),
             _pad_lanes(wl[:, o_a + DN_HEADS:o_a + 2 * DN_HEADS]), wl[:, o_a + 2 * DN_HEADS:]],
            axis=1).astype(BF16)
        params = (jnp.zeros((8, width), F32).at[0].set(gm_ln_g[l]).at[1].set(gm_ln_b[l])
                  .at[2, :DN_HEADS].set(a_log[l]).at[3, :DN_HEADS].set(dt_bias[l]))
        bsf = jnp.repeat(gm_b_s[l].T, width // GM_GROUPS, axis=1)
        oa, qkv, sz, gate, gb = _mix_in(
            h, w_pack, conv_w[l], params, gm_w_s[l], bsf, tm=tm_mix)
        x = _dn_out(qkv, gb, oa, sz, gate, x, mod3, dn_norm_g[l].reshape(1, -1),
                    w_branch[l, 0].astype(BF16), w_branch[l, 1].astype(BF16),
                    w_out[l].astype(BF16), tm=tm_dn)

        x = _ffn(x, mod3, norm3_g[l], ffn2_w_in[l].astype(BF16), ffn2_w_out[l].astype(BF16),
                 final_g, k=2, post_k=None, tm=tm_ffn)
    return x
```
